```python
import math
import jax
import jax.numpy as jnp
from jax import lax
import numpy as np

D_MODEL = 1024
BATCH = 8
SEQ = 8192
DEPTH = 2

MIX_A = D_MODEL // 2
RET_HEADS = 4
RET_HEAD_DIM = (D_MODEL - MIX_A) // RET_HEADS
RET_WIDTH = RET_HEADS * RET_HEAD_DIM
IN_COLS = 2 * MIX_A + 4 * RET_WIDTH
CONV_WIDTH = 31
RET_CHUNK = 128
ROPE_BASE = 10000.0
S5_GROUP = 16
S5_GROUPS = D_MODEL // S5_GROUP
S5_STATE = 64
D_FF = 4 * D_MODEL
N_EVEN = (DEPTH + 1) // 2
N_ODD = DEPTH // 2
EPS = 1e-6

kernel_name = "hybrid_conv_retention_s5_adaln_encoder"


def _rmsnorm(x, g):
    xf = x.astype(jnp.float32)
    y = xf * lax.rsqrt(jnp.mean(xf * xf, axis=-1, keepdims=True) + EPS)
    return (y * g.astype(jnp.float32)).astype(x.dtype)


def _head_rmsnorm(x):
    xf = x.astype(jnp.float32)
    y = xf * lax.rsqrt(jnp.mean(xf * xf, axis=-1, keepdims=True) + EPS)
    return y.astype(x.dtype)


def _layernorm(x, g, b):
    xf = x.astype(jnp.float32)
    mu = jnp.mean(xf, axis=-1, keepdims=True)
    xc = xf - mu
    var = jnp.mean(xc * xc, axis=-1, keepdims=True)
    y = xc * lax.rsqrt(var + EPS) * g.astype(jnp.float32) + b.astype(jnp.float32)
    return y.astype(x.dtype)


def _rotary(t, cos, sin):
    half = t.shape[-1] // 2
    t1, t2 = t[..., :half], t[..., half:]
    cos = cos.astype(t.dtype)
    sin = sin.astype(t.dtype)
    return jnp.concatenate([t1 * cos - t2 * sin, t1 * sin + t2 * cos], axis=-1)


def _retention_bidir(q, k, v):
    bsz, seq, nh, dh = q.shape
    L = RET_CHUNK
    n = seq // L
    dt = q.dtype
    q = q.reshape(bsz, n, L, nh, dh)
    k = k.reshape(bsz, n, L, nh, dh)
    v = v.reshape(bsz, n, L, nh, dh)
    log_g = jnp.log1p(-jnp.exp2(-5.0 - jnp.arange(nh, dtype=jnp.float32)))
    idx = jnp.arange(L, dtype=jnp.float32)
    dist = jnp.abs(idx[:, None] - idx[None, :])
    dmat = jnp.exp(log_g[:, None, None] * dist).astype(dt)
    scores = jnp.einsum('bnihd,bnjhd->bnhij', q, k) * dmat[None, None]
    intra = jnp.einsum('bnhij,bnjhe->bnihe', scores, v)
    w_kf = jnp.exp(log_g[None, :] * (L - 1.0 - idx)[:, None]).astype(dt)
    w_kb = jnp.exp(log_g[None, :] * idx[:, None]).astype(dt)
    kv_f = jnp.einsum('bnjhd,jh,bnjhe->nbhde', k, w_kf, v)
    kv_b = jnp.einsum('bnjhd,jh,bnjhe->nbhde', k, w_kb, v)
    g_L = jnp.exp(log_g * L).astype(dt)[None, :, None, None]

    def step(state, kv):
        return g_L * state + kv, state

    init = jnp.zeros((bsz, nh, dh, dh), dtype=kv_f.dtype)
    _, s_f = lax.scan(step, init, kv_f)
    _, s_b = lax.scan(step, init, kv_b, reverse=True)
    w_qf = jnp.exp(log_g[None, :] * (idx + 1.0)[:, None]).astype(dt)
    w_qb = jnp.exp(log_g[None, :] * (L - idx)[:, None]).astype(dt)
    inter = (jnp.einsum('bnihd,ih,nbhde->bnihe', q, w_qf, s_f)
             + jnp.einsum('bnihd,ih,nbhde->bnihe', q, w_qb, s_b))
    return (intra + inter).reshape(bsz, seq, nh, dh)


def _mixer_conv_retention(h, w_in, conv_w, conv_b, cln_g, cln_b, w_out):
    bsz, seq, _ = h.shape
    proj = h @ w_in
    splits = [MIX_A, 2 * MIX_A, 2 * MIX_A + RET_WIDTH,
              2 * MIX_A + 2 * RET_WIDTH, 2 * MIX_A + 3 * RET_WIDTH]
    a_val, a_gate, q, k, v, g = jnp.split(proj, splits, axis=-1)
    a = a_val * jax.nn.sigmoid(a_gate)
    a = lax.conv_general_dilated(
        a, conv_w[:, None, :].astype(a.dtype), window_strides=(1,), padding='SAME',
        dimension_numbers=('NWC', 'WIO', 'NWC'), feature_group_count=MIX_A)
    a = jax.nn.silu(_layernorm(a + conv_b, cln_g, cln_b))
    q = q.reshape(bsz, seq, RET_HEADS, RET_HEAD_DIM)
    k = k.reshape(bsz, seq, RET_HEADS, RET_HEAD_DIM)
    v = v.reshape(bsz, seq, RET_HEADS, RET_HEAD_DIM)
    inv = ROPE_BASE ** (-jnp.arange(0, RET_HEAD_DIM, 2, dtype=jnp.float32) / RET_HEAD_DIM)
    ang = jnp.arange(seq, dtype=jnp.float32)[:, None] * inv[None, :]
    cos, sin = jnp.cos(ang)[:, None, :], jnp.sin(ang)[:, None, :]
    q = _rotary(q, cos, sin) * (RET_HEAD_DIM ** -0.5)
    k = _rotary(k, cos, sin)
    r = _head_rmsnorm(_retention_bidir(q, k, v)).reshape(bsz, seq, RET_WIDTH)
    r = jax.nn.silu(g) * r
    return jnp.concatenate([a, r], axis=-1) @ w_out


def _s5_combine(e1, e2):
    a1r, a1i, b1r, b1i = e1
    a2r, a2i, b2r, b2i = e2
    ar = a2r * a1r - a2i * a1i
    ai = a2r * a1i + a2i * a1r
    br = a2r * b1r - a2i * b1i + b2r
    bi = a2r * b1i + a2i * b1r + b2i
    return ar, ai, br, bi


def _mixer_s5(h, lam_re, lam_im, log_step, b_re, b_im, c_re, c_im, d_skip, w_glu_a, w_glu_b):
    bsz, seq, dm = h.shape
    f32 = jnp.float32
    u = h.astype(f32)
    ug = u.reshape(bsz, seq, S5_GROUPS, S5_GROUP)
    lr = jnp.minimum(lam_re.astype(f32), -1e-4)
    li = lam_im.astype(f32)
    dt = jnp.exp(log_step.astype(f32))[..., None]
    mag = jnp.exp(lr * dt)
    ab_re, ab_im = mag * jnp.cos(li * dt), mag * jnp.sin(li * dt)
    den = lr * lr + li * li
    nr, ni = ab_re - 1.0, ab_im
    f_re = (nr * lr + ni * li) / den
    f_im = (ni * lr - nr * li) / den
    br_, bi_ = b_re.astype(f32), b_im.astype(f32)
    bb_re = f_re[..., None] * br_ - f_im[..., None] * bi_
    bb_im = f_re[..., None] * bi_ + f_im[..., None] * br_
    cr, ci = c_re.astype(f32), c_im.astype(f32)

    def one_sequence(us):
        y = jnp.zeros_like(us)
        for direction in (0, 1):
            bu_re = jnp.einsum('sgc,gpc->sgp', us, bb_re[direction])
            bu_im = jnp.einsum('sgc,gpc->sgp', us, bb_im[direction])
            a_re = jnp.broadcast_to(ab_re[direction], bu_re.shape)
            a_im = jnp.broadcast_to(ab_im[direction], bu_re.shape)
            _, _, h_re, h_im = lax.associative_scan(
                _s5_combine, (a_re, a_im, bu_re, bu_im), reverse=(direction == 1), axis=0)
            y = y + (jnp.einsum('sgp,gcp->sgc', h_re, cr[direction])
                     - jnp.einsum('sgp,gcp->sgc', h_im, ci[direction]))
        return y

    y = lax.map(one_sequence, ug).reshape(bsz, seq, dm)
    y = (y + d_skip.astype(f32) * u).astype(h.dtype)
    z = jax.nn.gelu(y)
    return (z @ w_glu_a) * jax.nn.sigmoid(z @ w_glu_b)


def setup_inputs(seed: int = 0) -> dict:
    key = jax.random.key(seed)
    ks = jax.random.split(key, 32)
    f32 = jnp.float32

    def nrm(k, shape, scale):
        return jax.random.normal(k, shape, f32) * scale

    P, G, Cg = S5_STATE, S5_GROUPS, S5_GROUP
    return {
        "x": nrm(ks[0], (BATCH, SEQ, D_MODEL), 1.0),
        "c": nrm(ks[1], (BATCH, D_MODEL), 1.0),
        "norm_g": 1.0 + nrm(ks[2], (DEPTH, 2, D_MODEL), 0.05),
        "ada_w": nrm(ks[3], (DEPTH, D_MODEL, 6 * D_MODEL), 0.5 * D_MODEL ** -0.5),
        "ada_b": nrm(ks[4], (DEPTH, 6 * D_MODEL), 0.02),
        "w_in": nrm(ks[5], (N_EVEN, D_MODEL, IN_COLS), D_MODEL ** -0.5),
        "conv_w": nrm(ks[6], (N_EVEN, CONV_WIDTH, MIX_A), CONV_WIDTH ** -0.5),
        "conv_b": nrm(ks[7], (N_EVEN, MIX_A), 0.02),
        "cln_g": 1.0 + nrm(ks[8], (N_EVEN, MIX_A), 0.05),
        "cln_b": nrm(ks[9], (N_EVEN, MIX_A), 0.02),
        "w_out": nrm(ks[10], (N_EVEN, MIX_A + RET_WIDTH, D_MODEL), (MIX_A + RET_WIDTH) ** -0.5),
        "s5_lam_re": -0.5 + nrm(ks[11], (N_ODD, 2, G, P), 0.01),
        "s5_lam_im": math.pi * jnp.arange(P, dtype=f32) + nrm(ks[12], (N_ODD, 2, G, P), 0.01),
        "s5_log_step": jax.random.uniform(ks[13], (N_ODD, 2, G), f32,
                                          math.log(1e-3), math.log(1e-1)),
        "s5_b_re": nrm(ks[14], (N_ODD, 2, G, P, Cg), (2 * Cg) ** -0.5),
        "s5_b_im": nrm(ks[15], (N_ODD, 2, G, P, Cg), (2 * Cg) ** -0.5),
        "s5_c_re": nrm(ks[16], (N_ODD, 2, G, Cg, P), 0.5),
        "s5_c_im": nrm(ks[17], (N_ODD, 2, G, Cg, P), 0.5),
        "s5_d": nrm(ks[18], (N_ODD, D_MODEL), 1.0),
        "w_glu_a": nrm(ks[19], (N_ODD, D_MODEL, D_MODEL), D_MODEL ** -0.5),
        "w_glu_b": nrm(ks[20], (N_ODD, D_MODEL, D_MODEL), D_MODEL ** -0.5),
        "w_fc1": nrm(ks[21], (DEPTH, D_MODEL, D_FF), D_MODEL ** -0.5),
        "w_fc2": nrm(ks[22], (DEPTH, D_FF, D_MODEL), D_FF ** -0.5),
        "norm_f": 1.0 + nrm(ks[23], (D_MODEL,), 0.05),
    }


def reference(x, c, norm_g, ada_w, ada_b, w_in, conv_w, conv_b, cln_g, cln_b, w_out,
              s5_lam_re, s5_lam_im, s5_log_step, s5_b_re, s5_b_im, s5_c_re, s5_c_im, s5_d,
              w_glu_a, w_glu_b, w_fc1, w_fc2, norm_f):
    c_act = jax.nn.silu(c)
    for layer in range(DEPTH):
        mod = (c_act @ ada_w[layer] + ada_b[layer])[:, None, :]
        shift_m, scale_m, gate_m, shift_f, scale_f, gate_f = jnp.split(mod, 6, axis=-1)
        h = _rmsnorm(x, norm_g[layer, 0]) * (1.0 + scale_m) + shift_m
        if layer % 2 == 0:
            i = layer // 2
            mix = _mixer_conv_retention(h, w_in[i], conv_w[i], conv_b[i], cln_g[i],
                                        cln_b[i], w_out[i])
        else:
            i = layer // 2
            mix = _mixer_s5(h, s5_lam_re[i], s5_lam_im[i], s5_log_step[i], s5_b_re[i],
                            s5_b_im[i], s5_c_re[i], s5_c_im[i], s5_d[i],
                            w_glu_a[i], w_glu_b[i])
        x = x + gate_m * mix
        h = _rmsnorm(x, norm_g[layer, 1]) * (1.0 + scale_f) + shift_f
        x = x + gate_f * (jnp.square(jax.nn.relu(h @ w_fc1[layer])) @ w_fc2[layer])
    return _rmsnorm(x, norm_f)
```

```python
import functools

import numpy as np
import jax
import jax.numpy as jnp
from jax import lax
from jax.experimental import pallas as pl
from jax.experimental.pallas import tpu as pltpu

F32 = jnp.float32
BF16 = jnp.bfloat16

D_MODEL = 1024
MIX_A = D_MODEL // 2
RET_HEADS = 4
RET_HEAD_DIM = 128
RET_WIDTH = RET_HEADS * RET_HEAD_DIM
IN_COLS = 2 * MIX_A + 4 * RET_WIDTH
CONV_WIDTH = 31
CONV_HALO = 16
ROPE_BASE = 10000.0
S5_GROUP = 16
S5_GROUPS = D_MODEL // S5_GROUP
S5_STATE = 64
S5_NSTATE = S5_GROUPS * S5_STATE
D_FF = 4 * D_MODEL
EPS = 1e-6

SUBLANES = 8
MXU_K = 256
T_RET = 512
T_MLP = 512
S5_STEPS = 32
S5_COLS = 512
CONV_ROWS = 32
VMEM_LIMIT = 56 * 1024 * 1024


def _cparams(sem):
    return pltpu.CompilerParams(dimension_semantics=sem, vmem_limit_bytes=VMEM_LIMIT)


def _const_spec(shape):
    nd = len(shape)
    return pl.BlockSpec(shape, lambda *_: (0,) * nd, pipeline_mode=pl.Buffered(1))


def _rms(x):
    return x * lax.rsqrt(jnp.mean(x * x, axis=-1, keepdims=True) + EPS)


def _dot(a, b):
    return jnp.dot(a, b, preferred_element_type=F32)


def _mod_kernel(c_ref, w_ref, b_ref, o_ref):
    c = c_ref[...]
    ca = c * jax.nn.sigmoid(c)
    w = w_ref[0]
    ch = ca.astype(BF16)
    cl = (ca - ch.astype(F32)).astype(BF16)
    wh = w.astype(BF16)
    wl = (w - wh.astype(F32)).astype(BF16)
    o_ref[0] = _dot(ch, wh) + _dot(cl, wh) + _dot(ch, wl) + b_ref[0]


def _modulation(c, ada_w, ada_b):
    depth, d, d6 = ada_w.shape
    bsz = c.shape[0]
    nblk = d6 // d
    return pl.pallas_call(
        _mod_kernel,
        grid=(depth, nblk),
        in_specs=[
            pl.BlockSpec((bsz, d), lambda l, j: (0, 0)),
            pl.BlockSpec((1, d, d), lambda l, j: (l, 0, j)),
            pl.BlockSpec((1, 1, d), lambda l, j: (l, 0, j)),
        ],
        out_specs=pl.BlockSpec((1, bsz, d), lambda l, j: (l, 0, j)),
        out_shape=jax.ShapeDtypeStruct((depth, bsz, d6), F32),
        compiler_params=_cparams(("arbitrary", "arbitrary")),
        name="adaln_mod",
    )(c, ada_w, ada_b.reshape(depth, 1, d6))


def _retention_tables(tile):
    h = np.arange(RET_HEADS, dtype=np.float64)
    log_g = np.log1p(-np.exp2(-5.0 - h))
    idx = np.arange(tile, dtype=np.float64)
    dist = np.abs(idx[:, None] - idx[None, :])
    dmat = np.exp(log_g[:, None, None] * dist)

    def lanes(w):
        return np.broadcast_to(w[:, :, None], (RET_HEADS, tile, RET_HEAD_DIM))

    w_kf = lanes(np.exp(log_g[:, None] * (tile - 1.0 - idx)[None, :]))
    w_kb = lanes(np.exp(log_g[:, None] * idx[None, :]))
    w_qf = lanes(np.exp(log_g[:, None] * (idx + 1.0)[None, :]))
    w_qb = lanes(np.exp(log_g[:, None] * (tile - idx)[None, :]))
    g_tile = np.exp(log_g * tile)
    f = lambda a: jnp.asarray(np.ascontiguousarray(a), dtype=F32)
    return f(dmat), f(w_kf), f(w_kb), f(w_qf), f(w_qb), f(g_tile)


def _rotary_tables(seq):
    half = RET_HEAD_DIM // 2
    inv = ROPE_BASE ** (-np.arange(0, RET_HEAD_DIM, 2, dtype=np.float64) / RET_HEAD_DIM)
    ang = np.arange(seq, dtype=np.float64)[:, None] * inv[None, :]
    cos, sin = np.cos(ang), np.sin(ang)
    cosf = np.concatenate([cos, cos], axis=1)
    sinf = np.concatenate([-sin, sin], axis=1)
    assert cosf.shape == (seq, 2 * half)
    return jnp.asarray(cosf, dtype=F32), jnp.asarray(sinf, dtype=F32)


def _inproj_kernel(x_ref, mod_ref, ng_ref, win_ref, cos_ref, sin_ref, wkf_ref, wkb_ref,
                   a_ref, q_ref, k_ref, v_ref, g_ref, kvf_ref, kvb_ref):
    d = D_MODEL
    x = x_ref[0]
    mod = mod_ref[0]
    shift, scale = mod[:, 0:d], mod[:, d:2 * d]
    h = _rms(x) * ng_ref[...] * (1.0 + scale) + shift
    hb = h.astype(BF16)

    a_val = _dot(hb, win_ref[:, 0:MIX_A])
    a_gate = _dot(hb, win_ref[:, MIX_A:2 * MIX_A])
    a_ref[0] = a_val * jax.nn.sigmoid(a_gate)

    cosf = cos_ref[...]
    sinf = sin_ref[...]
    half = RET_HEAD_DIM // 2
    base = 2 * MIX_A
    for hd in range(RET_HEADS):
        lo = hd * RET_HEAD_DIM
        hi = lo + RET_HEAD_DIM
        qh = _dot(hb, win_ref[:, base + lo:base + hi])
        kh = _dot(hb, win_ref[:, base + RET_WIDTH + lo:base + RET_WIDTH + hi])
        vh = _dot(hb, win_ref[:, base + 2 * RET_WIDTH + lo:base + 2 * RET_WIDTH + hi])
        gh = _dot(hb, win_ref[:, base + 3 * RET_WIDTH + lo:base + 3 * RET_WIDTH + hi])
        qh = (qh * cosf + pltpu.roll(qh, half, 1) * sinf) * (RET_HEAD_DIM ** -0.5)
        kh = kh * cosf + pltpu.roll(kh, half, 1) * sinf
        vb = vh.astype(BF16)
        q_ref[0, :, lo:hi] = qh.astype(BF16)
        k_ref[0, :, lo:hi] = kh.astype(BF16)
        v_ref[0, :, lo:hi] = vb
        g_ref[0, :, lo:hi] = gh.astype(BF16)
        tn = (((0,), (0,)), ((), ()))
        kvf_ref[0, 0, hd] = lax.dot_general((kh * wkf_ref[hd]).astype(BF16), vb, tn,
                                            preferred_element_type=F32)
        kvb_ref[0, 0, hd] = lax.dot_general((kh * wkb_ref[hd]).astype(BF16), vb, tn,
                                            preferred_element_type=F32)


def _inproj(x, mod0, ng, w_in, cosf, sinf, w_kf, w_kb):
    bsz, seq, d = x.shape
    t = T_RET
    nt = seq // t
    hd = RET_HEAD_DIM
    row = lambda b, i: (b, i, 0)
    kv_spec = pl.BlockSpec((1, 1, RET_HEADS, hd, hd), lambda b, i: (b, i, 0, 0, 0))
    kv_shape = jax.ShapeDtypeStruct((bsz, nt, RET_HEADS, hd, hd), F32)
    return pl.pallas_call(
        _inproj_kernel,
        grid=(bsz, nt),
        in_specs=[
            pl.BlockSpec((1, t, d), row),
            pl.BlockSpec((1, 1, 6 * d), lambda b, i: (b, 0, 0)),
            _const_spec((1, d)),
            _const_spec((d, IN_COLS)),
            pl.BlockSpec((t, hd), lambda b, i: (i, 0)),
            pl.BlockSpec((t, hd), lambda b, i: (i, 0)),
            _const_spec((RET_HEADS, t, hd)),
            _const_spec((RET_HEADS, t, hd)),
        ],
        out_specs=[
            pl.BlockSpec((1, t, MIX_A), row),
            pl.BlockSpec((1, t, RET_WIDTH), row),
            pl.BlockSpec((1, t, RET_WIDTH), row),
            pl.BlockSpec((1, t, RET_WIDTH), row),
            pl.BlockSpec((1, t, RET_WIDTH), row),
            kv_spec, kv_spec,
        ],
        out_shape=[
            jax.ShapeDtypeStruct((bsz, seq, MIX_A), F32),
            jax.ShapeDtypeStruct((bsz, seq, RET_WIDTH), BF16),
            jax.ShapeDtypeStruct((bsz, seq, RET_WIDTH), BF16),
            jax.ShapeDtypeStruct((bsz, seq, RET_WIDTH), BF16),
            jax.ShapeDtypeStruct((bsz, seq, RET_WIDTH), BF16),
            kv_shape, kv_shape,
        ],
        compiler_params=_cparams(("arbitrary", "arbitrary")),
        name="l0_inproj",
    )(x, mod0, ng, w_in, cosf, sinf, w_kf, w_kb)


def _retstate_kernel(gl_ref, kvf_ref, kvb_ref, s_ref):
    g = gl_ref[pl.program_id(1)]
    nt = kvf_ref.shape[1]
    hd = RET_HEAD_DIM
    s = jnp.zeros((hd, hd), F32)
    for c in range(nt):
        s_ref[0, c, 0, :, 0:hd] = s.astype(BF16)
        s = g * s + kvf_ref[0, c, 0]
    s = jnp.zeros((hd, hd), F32)
    for c in reversed(range(nt)):
        s_ref[0, c, 0, :, hd:2 * hd] = s.astype(BF16)
        s = g * s + kvb_ref[0, c, 0]


def _retstate(g_tile, kvf, kvb):
    bsz, nt, nh, hd, _ = kvf.shape
    kv_spec = pl.BlockSpec((1, nt, 1, hd, hd), lambda b, h: (b, 0, h, 0, 0))
    return pl.pallas_call(
        _retstate_kernel,
        grid=(bsz, nh),
        in_specs=[pl.BlockSpec(memory_space=pltpu.SMEM), kv_spec, kv_spec],
        out_specs=pl.BlockSpec((1, nt, 1, hd, 2 * hd), lambda b, h: (b, 0, h, 0, 0)),
        out_shape=jax.ShapeDtypeStruct((bsz, nt, nh, hd, 2 * hd), BF16),
        compiler_params=_cparams(("arbitrary", "arbitrary")),
        name="l0_retstate",
    )(g_tile, kvf, kvb)


def _mixer0_kernel(x_ref, mod_ref, a_ref, ap_ref, an_ref, q_ref, k_ref, v_ref, g_ref, s_ref,
                   dmat_ref, wqf_ref, wqb_ref, cw_ref, cb_ref, lg_ref, lb_ref, wout_ref,
                   o_ref, aext, cat):
    d = D_MODEL
    t = a_ref.shape[1]
    i = pl.program_id(1)
    last = pl.num_programs(1) - 1
    halo = CONV_HALO

    aext[0:halo, :] = jnp.where(i > 0, ap_ref[0], 0.0)
    aext[halo:halo + t, :] = a_ref[0]
    aext[halo + t:halo + t + halo, :] = jnp.where(i < last, an_ref[0], 0.0)

    off = halo - CONV_WIDTH // 2
    cb = cb_ref[...]
    lg = lg_ref[...]
    lb = lb_ref[...]
    for r0 in range(0, t, CONV_ROWS):
        acc = jnp.zeros((CONV_ROWS, MIX_A), F32)
        for kk in range(CONV_WIDTH):
            acc = acc + aext[r0 + off + kk:r0 + off + kk + CONV_ROWS, :] * cw_ref[kk:kk + 1, :]
        acc = acc + cb
        mu = jnp.mean(acc, axis=-1, keepdims=True)
        xc = acc - mu
        var = jnp.mean(xc * xc, axis=-1, keepdims=True)
        y = xc * lax.rsqrt(var + EPS) * lg + lb
        cat[r0:r0 + CONV_ROWS, 0:MIX_A] = (y * jax.nn.sigmoid(y)).astype(BF16)

    hd = RET_HEAD_DIM
    nt_dims = (((1,), (1,)), ((), ()))
    for h in range(RET_HEADS):
        lo = h * hd
        qh = q_ref[0, :, lo:lo + hd]
        kh = k_ref[0, :, lo:lo + hd]
        vh = v_ref[0, :, lo:lo + hd]
        sc = lax.dot_general(qh, kh, nt_dims, preferred_element_type=F32)
        p = (sc * dmat_ref[h]).astype(BF16)
        r = _dot(p, vh)
        cross = _dot(qh, s_ref[0, 0, h])
        r = r + wqf_ref[h] * cross[:, 0:hd] + wqb_ref[h] * cross[:, hd:2 * hd]
        r = _rms(r)
        gh = g_ref[0, :, lo:lo + hd].astype(F32)
        cat[:, MIX_A + lo:MIX_A + lo + hd] = (gh * jax.nn.sigmoid(gh) * r).astype(BF16)

    gate = mod_ref[0][:, 2 * d:3 * d]
    o_ref[0] = x_ref[0] + gate * _dot(cat[...], wout_ref[...])


def _mixer0(x, mod0, a, q, k, v, g, states, dmat, w_qf, w_qb, conv_w, conv_b, cln_g, cln_b,
            w_out):
    bsz, seq, d = x.shape
    t = T_RET
    nt = seq // t
    hd = RET_HEAD_DIM
    hb = t // CONV_HALO
    nhalo = seq // CONV_HALO
    row = lambda b, i: (b, i, 0)
    return pl.pallas_call(
        _mixer0_kernel,
        grid=(bsz, nt),
        in_specs=[
            pl.BlockSpec((1, t, d), row),
            pl.BlockSpec((1, 1, 6 * d), lambda b, i: (b, 0, 0)),
            pl.BlockSpec((1, t, MIX_A), row),
            pl.BlockSpec((1, CONV_HALO, MIX_A), lambda b, i: (b, jnp.maximum(i * hb - 1, 0), 0)),
            pl.BlockSpec((1, CONV_HALO, MIX_A),
                         lambda b, i: (b, jnp.minimum((i + 1) * hb, nhalo - 1), 0)),
            pl.BlockSpec((1, t, RET_WIDTH), row),
            pl.BlockSpec((1, t, RET_WIDTH), row),
            pl.BlockSpec((1, t, RET_WIDTH), row),
            pl.BlockSpec((1, t, RET_WIDTH), row),
            pl.BlockSpec((1, 1, RET_HEADS, hd, 2 * hd), lambda b, i: (b, i, 0, 0, 0)),
            _const_spec((RET_HEADS, t, t)),
            _const_spec((RET_HEADS, t, hd)),
            _const_spec((RET_HEADS, t, hd)),
            _const_spec((CONV_WIDTH + 1, MIX_A)),
            _const_spec((1, MIX_A)),
            _const_spec((1, MIX_A)),
            _const_spec((1, MIX_A)),
            _const_spec((d, d)),
        ],
        out_specs=pl.BlockSpec((1, t, d), row),
        out_shape=jax.ShapeDtypeStruct((bsz, seq, d), F32),
        scratch_shapes=[
            pltpu.VMEM((t + 2 * CONV_HALO, MIX_A), F32),
            pltpu.VMEM((t, d), BF16),
        ],
        compiler_params=_cparams(("arbitrary", "arbitrary")),
        name="l0_mixer",
    )(x, mod0, a, a, a, q, k, v, g, states, dmat, w_qf, w_qb, conv_w, conv_b, cln_g, cln_b,
      w_out)


def _mlp_kernel(x_ref, mod_ref, ng_ref, w1_ref, w2_ref, nf_ref, o_ref, *, final_norm):
    d = D_MODEL
    t = T_MLP
    x = x_ref[...].reshape(t, d)
    mod = mod_ref[0]
    shift, scale, gate = mod[:, 3 * d:4 * d], mod[:, 4 * d:5 * d], mod[:, 5 * d:6 * d]
    hb = (_rms(x) * ng_ref[...] * (1.0 + scale) + shift).astype(BF16)
    acc = jnp.zeros((t, d), F32)
    for j in range(0, D_FF, d):
        u = jnp.maximum(_dot(hb, w1_ref[:, j:j + d]), 0.0)
        acc = acc + _dot((u * u).astype(BF16), w2_ref[j:j + d, :])
    y = x + gate * acc
    if final_norm:
        y = _rms(y) * nf_ref[...]
    o_ref[...] = y.reshape(o_ref.shape)


def _mlp(x, mod, ng, w1, w2, nf, *, bsz, seq, in_time_major, final_norm):
    d = D_MODEL
    t = T_MLP
    bm_spec = pl.BlockSpec((1, t, d), lambda b, i: (b, i, 0))
    tm_spec = pl.BlockSpec((t, d), lambda b, i: (i, b))
    bm_shape = jax.ShapeDtypeStruct((bsz, seq, d), F32)
    tm_shape = jax.ShapeDtypeStruct((seq, bsz * d), F32)
    return pl.pallas_call(
        functools.partial(_mlp_kernel, final_norm=final_norm),
        grid=(bsz, seq // t),
        in_specs=[
            tm_spec if in_time_major else bm_spec,
            pl.BlockSpec((1, 1, 6 * d), lambda b, i: (b, 0, 0)),
            _const_spec((1, d)),
            _const_spec((d, D_FF)),
            _const_spec((D_FF, d)),
            _const_spec((1, d)),
        ],
        out_specs=bm_spec if in_time_major else tm_spec,
        out_shape=bm_shape if in_time_major else tm_shape,
        compiler_params=_cparams(("arbitrary", "arbitrary")),
        name="mlp_final" if final_norm else "mlp",
    )(x, mod, ng, w1, w2, nf)


def _s5_weights(lam_re, lam_im, log_step, b_re, b_im, c_re, c_im):
    lr = jnp.minimum(lam_re.astype(F32), -1e-4)
    li = lam_im.astype(F32)
    dt = jnp.exp(log_step.astype(F32))[..., None]
    mag = jnp.exp(lr * dt)
    ab_re, ab_im = mag * jnp.cos(li * dt), mag * jnp.sin(li * dt)
    den = lr * lr + li * li
    nr, ni = ab_re - 1.0, ab_im
    f_re = (nr * lr + ni * li) / den
    f_im = (ni * lr - nr * li) / den
    br_, bi_ = b_re.astype(F32), b_im.astype(F32)
    bb_re = f_re[..., None] * br_ - f_im[..., None] * bi_
    bb_im = f_re[..., None] * bi_ + f_im[..., None] * br_

    gpk = MXU_K // S5_GROUP
    nkb = S5_GROUPS // gpk
    eye = jnp.eye(gpk, dtype=F32)

    def pack_b(bb):
        bb = bb.reshape(2, nkb, gpk, S5_STATE, S5_GROUP)
        w = jnp.einsum('dkgpc,gh->dkgchp', bb, eye)
        return w.reshape(2, nkb, MXU_K, gpk * S5_STATE)

    def pack_c(cc):
        cc = cc.astype(F32).reshape(2, nkb, gpk, S5_GROUP, S5_STATE)
        w = jnp.einsum('dkgcp,gh->dkgphc', cc, eye)
        return w.reshape(2, nkb, gpk * S5_STATE, MXU_K)

    wb = jnp.concatenate([pack_b(bb_re), pack_b(bb_im)], axis=-1).astype(BF16)
    wc = jnp.concatenate([pack_c(c_re), -pack_c(c_im)], axis=-2).astype(BF16)
    a_re = jnp.broadcast_to(ab_re.reshape(2, 1, S5_NSTATE), (2, SUBLANES, S5_NSTATE))
    a_im = jnp.broadcast_to(ab_im.reshape(2, 1, S5_NSTATE), (2, SUBLANES, S5_NSTATE))
    return a_re, a_im, wb, wc


def _s5_direction(x_ref, mod_ref, ng_ref, are_ref, aim_ref, wb_ref, wc_ref,
                  bur, bui, hre, him, *, reverse):
    d = D_MODEL
    rows = x_ref.shape[0]
    steps = rows // SUBLANES

    @pl.when(pl.program_id(0) == 0)
    def _():
        hre[...] = jnp.zeros_like(hre)
        him[...] = jnp.zeros_like(him)

    mod = mod_ref[...]
    shift, scale = mod[:, 0:d], mod[:, d:2 * d]
    xn = (_rms(x_ref[...]) * ng_ref[...]).reshape(steps, SUBLANES, d)
    u = (xn * (1.0 + scale)[None] + shift[None]).reshape(rows, d)
    ub = u.astype(BF16)

    blk = wb_ref.shape[3] // 2
    nkb = wb_ref.shape[1]
    for kb in range(nkb):
        r = _dot(ub[:, kb * MXU_K:(kb + 1) * MXU_K], wb_ref[0, kb])
        bur[:, kb * blk:(kb + 1) * blk] = r[:, 0:blk]
        bui[:, kb * blk:(kb + 1) * blk] = r[:, blk:2 * blk]

    for j in range(0, S5_NSTATE, S5_COLS):
        ar = are_ref[0, :, j:j + S5_COLS]
        ai = aim_ref[0, :, j:j + S5_COLS]

        def body(s, carry, j=j, ar=ar, ai=ai):
            hr, hi = carry
            tt = (steps - 1 - s) if reverse else s
            r0 = pl.multiple_of(tt * SUBLANES, SUBLANES)
            nr = ar * hr - ai * hi + bur[pl.ds(r0, SUBLANES), j:j + S5_COLS]
            ni = ar * hi + ai * hr + bui[pl.ds(r0, SUBLANES), j:j + S5_COLS]
            bur[pl.ds(r0, SUBLANES), j:j + S5_COLS] = nr
            bui[pl.ds(r0, SUBLANES), j:j + S5_COLS] = ni
            return nr, ni

        hr, hi = lax.fori_loop(0, steps, body, (hre[:, j:j + S5_COLS], him[:, j:j + S5_COLS]),
                               unroll=8)
        hre[:, j:j + S5_COLS] = hr
        him[:, j:j + S5_COLS] = hi

    ys = []
    for kb in range(nkb):
        hcat = jnp.concatenate([bur[:, kb * blk:(kb + 1) * blk].astype(BF16),
                                bui[:, kb * blk:(kb + 1) * blk].astype(BF16)], axis=1)
        ys.append(_dot(hcat, wc_ref[0, kb]))
    return u, jnp.concatenate(ys, axis=1)


def _s5_fwd_kernel(x_ref, mod_ref, ng_ref, are_ref, aim_ref, wb_ref, wc_ref, yf_ref,
                   bur, bui, hre, him):
    _, y = _s5_direction(x_ref, mod_ref, ng_ref, are_ref, aim_ref, wb_ref, wc_ref,
                         bur, bui, hre, him, reverse=False)
    yf_ref[...] = y


def _s5_bwd_kernel(x_ref, mod_ref, ng_ref, are_ref, aim_ref, wb_ref, wc_ref, yf_ref,
                   dsk_ref, wa_ref, wg_ref, o_ref, bur, bui, hre, him):
    d = D_MODEL
    rows = x_ref.shape[0]
    steps = rows // SUBLANES
    u, yb = _s5_direction(x_ref, mod_ref, ng_ref, are_ref, aim_ref, wb_ref, wc_ref,
                          bur, bui, hre, him, reverse=True)
    y = yf_ref[...] + yb + dsk_ref[...] * u
    zb = jax.nn.gelu(y).astype(BF16)
    glu = _dot(zb, wa_ref[...]) * jax.nn.sigmoid(_dot(zb, wg_ref[...]))
    gate = mod_ref[...][:, 2 * d:3 * d]
    out = x_ref[...].reshape(steps, SUBLANES, d) + gate[None] * glu.reshape(steps, SUBLANES, d)
    o_ref[...] = out.reshape(rows, d)


def _s5_layer(x_tm, mod1, ng, a_re, a_im, wb, wc, d_skip, w_glu_a, w_glu_b):
    n, d = x_tm.shape
    rows = S5_STEPS * SUBLANES
    nt = n // rows
    nkb = wb.shape[1]
    scratch = [
        pltpu.VMEM((rows, S5_NSTATE), F32),
        pltpu.VMEM((rows, S5_NSTATE), F32),
        pltpu.VMEM((SUBLANES, S5_NSTATE), F32),
        pltpu.VMEM((SUBLANES, S5_NSTATE), F32),
    ]

    def dir_specs(direction, order):
        return [
            pl.BlockSpec((rows, d), lambda i: (order(i), 0)),
            _const_spec((SUBLANES, 6 * d)),
            _const_spec((1, d)),
            pl.BlockSpec((1, SUBLANES, S5_NSTATE), lambda i: (direction, 0, 0),
                         pipeline_mode=pl.Buffered(1)),
            pl.BlockSpec((1, SUBLANES, S5_NSTATE), lambda i: (direction, 0, 0),
                         pipeline_mode=pl.Buffered(1)),
            pl.BlockSpec((1,) + wb.shape[1:], lambda i: (direction, 0, 0, 0),
                         pipeline_mode=pl.Buffered(1)),
            pl.BlockSpec((1,) + wc.shape[1:], lambda i: (direction, 0, 0, 0),
                         pipeline_mode=pl.Buffered(1)),
        ]

    fwd = lambda i: i
    bwd = lambda i: nt - 1 - i
    yf = pl.pallas_call(
        _s5_fwd_kernel,
        grid=(nt,),
        in_specs=dir_specs(0, fwd),
        out_specs=pl.BlockSpec((rows, d), lambda i: (i, 0)),
        out_shape=jax.ShapeDtypeStruct((n, d), F32),
        scratch_shapes=scratch,
        compiler_params=_cparams(("arbitrary",)),
        name="s5_fwd",
    )(x_tm, mod1, ng, a_re, a_im, wb, wc)
    return pl.pallas_call(
        _s5_bwd_kernel,
        grid=(nt,),
        in_specs=dir_specs(1, bwd) + [
            pl.BlockSpec((rows, d), lambda i: (bwd(i), 0)),
            _const_spec((1, d)),
            _const_spec((d, d)),
            _const_spec((d, d)),
        ],
        out_specs=pl.BlockSpec((rows, d), lambda i: (bwd(i), 0)),
        out_shape=jax.ShapeDtypeStruct((n, d), F32),
        scratch_shapes=scratch,
        compiler_params=_cparams(("arbitrary",)),
        name="s5_bwd",
    )(x_tm, mod1, ng, a_re, a_im, wb, wc, yf, d_skip, w_glu_a, w_glu_b)


def kernel(x, c, norm_g, ada_w, ada_b, w_in, conv_w, conv_b, cln_g, cln_b, w_out, s5_lam_re,
           s5_lam_im, s5_log_step, s5_b_re, s5_b_im, s5_c_re, s5_c_im, s5_d, w_glu_a, w_glu_b,
           w_fc1, w_fc2, norm_f):
    bsz, seq, d = x.shape
    assert d == D_MODEL and bsz == SUBLANES
    assert seq % T_RET == 0 and seq % T_MLP == 0 and seq % S5_STEPS == 0
    assert norm_g.shape[0] == 2, "one conv/retention layer followed by one S5 layer"

    mod = _modulation(c, ada_w, ada_b)
    mod0 = mod[0].reshape(bsz, 1, 6 * d)
    mod1 = mod[1]
    row = lambda v: v.reshape(1, -1).astype(F32)

    dmat, w_kf, w_kb, w_qf, w_qb, g_tile = _retention_tables(T_RET)
    cosf, sinf = _rotary_tables(seq)
    a, q, k, v, g, kvf, kvb = _inproj(x, mod0, row(norm_g[0, 0]), w_in[0].astype(BF16),
                                      cosf, sinf, w_kf, w_kb)
    states = _retstate(g_tile, kvf, kvb)
    cw = jnp.concatenate([conv_w[0], jnp.zeros((1, MIX_A), conv_w.dtype)], axis=0)
    x = _mixer0(x, mod0, a, q, k, v, g, states, dmat, w_qf, w_qb, cw, row(conv_b[0]),
                row(cln_g[0]), row(cln_b[0]), w_out[0].astype(BF16))
    x = _mlp(x, mod0, row(norm_g[0, 1]), w_fc1[0].astype(BF16), w_fc2[0].astype(BF16),
             row(norm_f), bsz=bsz, seq=seq, in_time_major=False, final_norm=False)

    a_re, a_im, wb, wc = _s5_weights(s5_lam_re[0], s5_lam_im[0], s5_log_step[0], s5_b_re[0],
                                     s5_b_im[0], s5_c_re[0], s5_c_im[0])
    x = _s5_layer(x.reshape(seq * bsz, d), mod1, row(norm_g[1, 0]), a_re, a_im, wb, wc,
                  row(s5_d[0]), w_glu_a[0].astype(BF16), w_glu_b[0].astype(BF16))
    return _mlp(x.reshape(seq, bsz * d), mod.reshape(2, bsz, 1, 6 * d)[1], row(norm_g[1, 1]),
                w_fc1[1].astype(BF16), w_fc2[1].astype(BF16), row(norm_f),
                bsz=bsz, seq=seq, in_time_major=True, final_norm=True)
```

```python
import numpy as np
import jax
import jax.numpy as jnp
from jax import lax
from jax.experimental import pallas as pl
from jax.experimental.pallas import tpu as pltpu

F32 = jnp.float32
BF16 = jnp.bfloat16

D_MODEL = 1024
MIX_A = D_MODEL // 2
RET_HEADS = 4
RET_HEAD_DIM = 128
RET_WIDTH = RET_HEADS * RET_HEAD_DIM
IN_COLS = 2 * MIX_A + 4 * RET_WIDTH
CONV_WIDTH = 31
CONV_HALO = 16
ROPE_BASE = 10000.0
S5_GROUP = 16
S5_GROUPS = D_MODEL // S5_GROUP
S5_STATE = 64
S5_NSTATE = S5_GROUPS * S5_STATE
D_FF = 4 * D_MODEL
EPS = 1e-6

SUBLANES = 8
MXU_K = 256
T_RET = 512
MLP_STEPS = 64
S5_STEPS = 32
S5_COLS = 512
CONV_ROWS = 32
CONV_COPY_ROWS = 64
VMEM_LIMIT = 56 * 1024 * 1024


def _cparams(sem):
    return pltpu.CompilerParams(dimension_semantics=sem, vmem_limit_bytes=VMEM_LIMIT)


def _const_spec(shape):
    nd = len(shape)
    return pl.BlockSpec(shape, lambda *_: (0,) * nd, pipeline_mode=pl.Buffered(1))


def _rms(x):
    return x * lax.rsqrt(jnp.mean(x * x, axis=-1, keepdims=True) + EPS)


def _dot(a, b):
    return jnp.dot(a, b, preferred_element_type=F32)


def _ffn(hb, w1_ref, w2_ref):
    d = D_MODEL
    acc = jnp.zeros((hb.shape[0], d), F32)
    for j in range(0, D_FF, d):
        u = jnp.maximum(_dot(hb, w1_ref[:, j:j + d]), 0.0)
        acc = acc + _dot((u * u).astype(BF16), w2_ref[j:j + d, :])
    return acc


def _mod_kernel(c_ref, w_ref, b_ref, o_ref):
    c = c_ref[...]
    ca = c * jax.nn.sigmoid(c)
    w = w_ref[0]
    ch = ca.astype(BF16)
    cl = (ca - ch.astype(F32)).astype(BF16)
    wh = w.astype(BF16)
    wl = (w - wh.astype(F32)).astype(BF16)
    o_ref[0] = _dot(ch, wh) + _dot(cl, wh) + _dot(ch, wl) + b_ref[0]


def _modulation(c, ada_w, ada_b):
    depth, d, d6 = ada_w.shape
    bsz = c.shape[0]
    nblk = d6 // d
    return pl.pallas_call(
        _mod_kernel,
        grid=(depth, nblk),
        in_specs=[
            pl.BlockSpec((bsz, d), lambda l, j: (0, 0)),
            pl.BlockSpec((1, d, d), lambda l, j: (l, 0, j)),
            pl.BlockSpec((1, 1, d), lambda l, j: (l, 0, j)),
        ],
        out_specs=pl.BlockSpec((1, bsz, d), lambda l, j: (l, 0, j)),
        out_shape=jax.ShapeDtypeStruct((depth, bsz, d6), F32),
        compiler_params=_cparams(("arbitrary", "arbitrary")),
        name="adaln_mod",
    )(c, ada_w, ada_b.reshape(depth, 1, d6))


def _retention_tables(tile):
    h = np.arange(RET_HEADS, dtype=np.float64)
    log_g = np.log1p(-np.exp2(-5.0 - h))
    idx = np.arange(tile, dtype=np.float64)
    dist = np.abs(idx[:, None] - idx[None, :])
    dmat = np.exp(log_g[:, None, None] * dist)

    def lanes(w):
        return np.broadcast_to(w[:, :, None], (RET_HEADS, tile, RET_HEAD_DIM))

    w_kf = lanes(np.exp(log_g[:, None] * (tile - 1.0 - idx)[None, :]))
    w_kb = lanes(np.exp(log_g[:, None] * idx[None, :]))
    w_qf = lanes(np.exp(log_g[:, None] * (idx + 1.0)[None, :]))
    w_qb = lanes(np.exp(log_g[:, None] * (tile - idx)[None, :]))
    g_tile = np.exp(log_g * tile)
    f = lambda a: jnp.asarray(np.ascontiguousarray(a), dtype=F32)
    return f(dmat), f(w_kf), f(w_kb), f(w_qf), f(w_qb), f(g_tile)


def _rotary_tables(seq):
    half = RET_HEAD_DIM // 2
    inv = ROPE_BASE ** (-np.arange(0, RET_HEAD_DIM, 2, dtype=np.float64) / RET_HEAD_DIM)
    ang = np.arange(seq, dtype=np.float64)[:, None] * inv[None, :]
    cos, sin = np.cos(ang), np.sin(ang)
    cosf = np.concatenate([cos, cos], axis=1)
    sinf = np.concatenate([-sin, sin], axis=1)
    assert cosf.shape == (seq, 2 * half)
    return jnp.asarray(cosf, dtype=F32), jnp.asarray(sinf, dtype=F32)


def _inproj_kernel(x_ref, mod_ref, ng_ref, win_ref, cos_ref, sin_ref, wkf_ref, wkb_ref,
                   a_ref, q_ref, k_ref, v_ref, g_ref, kvf_ref, kvb_ref):
    d = D_MODEL
    x = x_ref[0]
    mod = mod_ref[0]
    shift, scale = mod[:, 0:d], mod[:, d:2 * d]
    h = _rms(x) * ng_ref[...] * (1.0 + scale) + shift
    hb = h.astype(BF16)

    a_val = _dot(hb, win_ref[:, 0:MIX_A])
    a_gate = _dot(hb, win_ref[:, MIX_A:2 * MIX_A])
    a_ref[0] = a_val * jax.nn.sigmoid(a_gate)

    cosf = cos_ref[...]
    sinf = sin_ref[...]
    half = RET_HEAD_DIM // 2
    base = 2 * MIX_A
    g_ref[0] = _dot(hb, win_ref[:, base + 3 * RET_WIDTH:base + 4 * RET_WIDTH]).astype(BF16)
    v_all = _dot(hb, win_ref[:, base + 2 * RET_WIDTH:base + 3 * RET_WIDTH]).astype(BF16)
    v_ref[0] = v_all
    q_all = _dot(hb, win_ref[:, base:base + RET_WIDTH])
    k_all = _dot(hb, win_ref[:, base + RET_WIDTH:base + 2 * RET_WIDTH])
    for hd in range(RET_HEADS):
        lo = hd * RET_HEAD_DIM
        hi = lo + RET_HEAD_DIM
        qh = q_all[:, lo:hi]
        kh = k_all[:, lo:hi]
        qh = (qh * cosf + pltpu.roll(qh, half, 1) * sinf) * (RET_HEAD_DIM ** -0.5)
        kh = kh * cosf + pltpu.roll(kh, half, 1) * sinf
        vb = v_all[:, lo:hi]
        q_ref[0, :, lo:hi] = qh.astype(BF16)
        k_ref[0, :, lo:hi] = kh.astype(BF16)
        tn = (((0,), (0,)), ((), ()))
        kvf_ref[0, 0, hd] = lax.dot_general((kh * wkf_ref[hd]).astype(BF16), vb, tn,
                                            preferred_element_type=F32)
        kvb_ref[0, 0, hd] = lax.dot_general((kh * wkb_ref[hd]).astype(BF16), vb, tn,
                                            preferred_element_type=F32)


def _inproj(x, mod0, ng, w_in, cosf, sinf, w_kf, w_kb):
    bsz, seq, d = x.shape
    t = T_RET
    nt = seq // t
    hd = RET_HEAD_DIM
    row = lambda b, i: (b, i, 0)
    kv_spec = pl.BlockSpec((1, 1, RET_HEADS, hd, hd), lambda b, i: (b, i, 0, 0, 0))
    kv_shape = jax.ShapeDtypeStruct((bsz, nt, RET_HEADS, hd, hd), F32)
    return pl.pallas_call(
        _inproj_kernel,
        grid=(bsz, nt),
        in_specs=[
            pl.BlockSpec((1, t, d), row),
            pl.BlockSpec((1, 1, 6 * d), lambda b, i: (b, 0, 0)),
            _const_spec((1, d)),
            _const_spec((d, IN_COLS)),
            pl.BlockSpec((t, hd), lambda b, i: (i, 0)),
            pl.BlockSpec((t, hd), lambda b, i: (i, 0)),
            _const_spec((RET_HEADS, t, hd)),
            _const_spec((RET_HEADS, t, hd)),
        ],
        out_specs=[
            pl.BlockSpec((1, t, MIX_A), row),
            pl.BlockSpec((1, t, RET_WIDTH), row),
            pl.BlockSpec((1, t, RET_WIDTH), row),
            pl.BlockSpec((1, t, RET_WIDTH), row),
            pl.BlockSpec((1, t, RET_WIDTH), row),
            kv_spec, kv_spec,
        ],
        out_shape=[
            jax.ShapeDtypeStruct((bsz, seq, MIX_A), F32),
            jax.ShapeDtypeStruct((bsz, seq, RET_WIDTH), BF16),
            jax.ShapeDtypeStruct((bsz, seq, RET_WIDTH), BF16),
            jax.ShapeDtypeStruct((bsz, seq, RET_WIDTH), BF16),
            jax.ShapeDtypeStruct((bsz, seq, RET_WIDTH), BF16),
            kv_shape, kv_shape,
        ],
        compiler_params=_cparams(("arbitrary", "arbitrary")),
        name="l0_inproj",
    )(x, mod0, ng, w_in, cosf, sinf, w_kf, w_kb)


def _retstate_kernel(gl_ref, kvf_ref, kvb_ref, s_ref):
    g = gl_ref[pl.program_id(1)]
    nt = kvf_ref.shape[1]
    hd = RET_HEAD_DIM
    s = jnp.zeros((hd, hd), F32)
    for c in range(nt):
        s_ref[0, c, 0, :, 0:hd] = s.astype(BF16)
        s = g * s + kvf_ref[0, c, 0]
    s = jnp.zeros((hd, hd), F32)
    for c in reversed(range(nt)):
        s_ref[0, c, 0, :, hd:2 * hd] = s.astype(BF16)
        s = g * s + kvb_ref[0, c, 0]


def _retstate(g_tile, kvf, kvb):
    bsz, nt, nh, hd, _ = kvf.shape
    kv_spec = pl.BlockSpec((1, nt, 1, hd, hd), lambda b, h: (b, 0, h, 0, 0))
    return pl.pallas_call(
        _retstate_kernel,
        grid=(bsz, nh),
        in_specs=[pl.BlockSpec(memory_space=pltpu.SMEM), kv_spec, kv_spec],
        out_specs=pl.BlockSpec((1, nt, 1, hd, 2 * hd), lambda b, h: (b, 0, h, 0, 0)),
        out_shape=jax.ShapeDtypeStruct((bsz, nt, nh, hd, 2 * hd), BF16),
        compiler_params=_cparams(("arbitrary", "arbitrary")),
        name="l0_retstate",
    )(g_tile, kvf, kvb)


def _mixer0_kernel(x_ref, mod_ref, a_ref, ap_ref, an_ref, q_ref, k_ref, v_ref, g_ref,
                   s_ref, dmat_ref, wqf_ref, wqb_ref, cw_ref, cb_ref, lg_ref, lb_ref, wout_ref,
                   ng2_ref, w1_ref, w2_ref, o_ref, aext, ash, cat):
    d = D_MODEL
    t = a_ref.shape[1]
    i = pl.program_id(1)
    last = pl.num_programs(1) - 1
    halo = CONV_HALO

    aext[0:halo, :] = jnp.where(i > 0, ap_ref[0], 0.0)
    aext[halo:halo + t, :] = a_ref[0]
    aext[halo + t:halo + t + halo, :] = jnp.where(i < last, an_ref[0], 0.0)

    off = halo - CONV_WIDTH // 2
    span = t + SUBLANES * ((off + CONV_WIDTH - 1) // SUBLANES)
    for s in range(1, SUBLANES):
        for r0 in range(0, span, CONV_COPY_ROWS):
            n = min(CONV_COPY_ROWS, span - r0)
            ash[s - 1, r0:r0 + n, :] = aext[r0 + s:r0 + s + n, :]

    cb = cb_ref[...]
    lg = lg_ref[...]
    lb = lb_ref[...]
    for r0 in range(0, t, CONV_ROWS):
        acc = jnp.zeros((CONV_ROWS, MIX_A), F32)
        for kk in range(CONV_WIDTH):
            s = (off + kk) % SUBLANES
            base = r0 + (off + kk) - s
            if s == 0:
                tap = aext[base:base + CONV_ROWS, :]
            else:
                tap = ash[s - 1, base:base + CONV_ROWS, :]
            acc = acc + tap * cw_ref[kk:kk + 1, :]
        acc = acc + cb
        mu = jnp.mean(acc, axis=-1, keepdims=True)
        xc = acc - mu
        var = jnp.mean(xc * xc, axis=-1, keepdims=True)
        y = xc * lax.rsqrt(var + EPS) * lg + lb
        cat[r0:r0 + CONV_ROWS, 0:MIX_A] = (y * jax.nn.sigmoid(y)).astype(BF16)

    hd = RET_HEAD_DIM
    nt_dims = (((1,), (1,)), ((), ()))
    for h in range(RET_HEADS):
        lo = h * hd
        qh = q_ref[0, :, lo:lo + hd]
        kh = k_ref[0, :, lo:lo + hd]
        vh = v_ref[0, :, lo:lo + hd]
        sc = lax.dot_general(qh, kh, nt_dims, preferred_element_type=F32)
        p = (sc * dmat_ref[h]).astype(BF16)
        r = _dot(p, vh)
        cross = _dot(qh, s_ref[0, 0, h])
        r = r + wqf_ref[h] * cross[:, 0:hd] + wqb_ref[h] * cross[:, hd:2 * hd]
        r = _rms(r)
        gh = g_ref[0, :, lo:lo + hd].astype(F32)
        cat[:, MIX_A + lo:MIX_A + lo + hd] = (gh * jax.nn.sigmoid(gh) * r).astype(BF16)

    mod = mod_ref[0]
    x1 = x_ref[0] + mod[:, 2 * d:3 * d] * _dot(cat[...], wout_ref[...])

    shift, scale, gate = mod[:, 3 * d:4 * d], mod[:, 4 * d:5 * d], mod[:, 5 * d:6 * d]
    hb = (_rms(x1) * ng2_ref[...] * (1.0 + scale) + shift).astype(BF16)
    o_ref[0] = x1 + gate * _ffn(hb, w1_ref, w2_ref)


def _mixer0(x, mod0, a, q, k, v, g, states, dmat, w_qf, w_qb, conv_w, conv_b, cln_g, cln_b,
            w_out, ng2, w1, w2):
    bsz, seq, d = x.shape
    t = T_RET
    nt = seq // t
    hd = RET_HEAD_DIM
    hb = t // CONV_HALO
    nhalo = seq // CONV_HALO
    row = lambda b, i: (b, i, 0)
    return pl.pallas_call(
        _mixer0_kernel,
        grid=(bsz, nt),
        in_specs=[
            pl.BlockSpec((1, t, d), row),
            pl.BlockSpec((1, 1, 6 * d), lambda b, i: (b, 0, 0)),
            pl.BlockSpec((1, t, MIX_A), row),
            pl.BlockSpec((1, CONV_HALO, MIX_A), lambda b, i: (b, jnp.maximum(i * hb - 1, 0), 0)),
            pl.BlockSpec((1, CONV_HALO, MIX_A),
                         lambda b, i: (b, jnp.minimum((i + 1) * hb, nhalo - 1), 0)),
            pl.BlockSpec((1, t, RET_WIDTH), row),
            pl.BlockSpec((1, t, RET_WIDTH), row),
            pl.BlockSpec((1, t, RET_WIDTH), row),
            pl.BlockSpec((1, t, RET_WIDTH), row),
            pl.BlockSpec((1, 1, RET_HEADS, hd, 2 * hd), lambda b, i: (b, i, 0, 0, 0)),
            _const_spec((RET_HEADS, t, t)),
            _const_spec((RET_HEADS, t, hd)),
            _const_spec((RET_HEADS, t, hd)),
            _const_spec((CONV_WIDTH + 1, MIX_A)),
            _const_spec((1, MIX_A)),
            _const_spec((1, MIX_A)),
            _const_spec((1, MIX_A)),
            _const_spec((d, d)),
            _const_spec((1, d)),
            _const_spec((d, D_FF)),
            _const_spec((D_FF, d)),
        ],
        out_specs=pl.BlockSpec((1, t, d), row),
        out_shape=jax.ShapeDtypeStruct((bsz, seq, d), F32),
        scratch_shapes=[
            pltpu.VMEM((t + 2 * CONV_HALO, MIX_A), F32),
            pltpu.VMEM((SUBLANES - 1, t + 2 * CONV_HALO, MIX_A), F32),
            pltpu.VMEM((t, d), BF16),
        ],
        compiler_params=_cparams(("arbitrary", "arbitrary")),
        name="l0_mixer_mlp",
    )(x, mod0, a, a, a, q, k, v, g, states, dmat, w_qf, w_qb, conv_w, conv_b, cln_g, cln_b,
      w_out, ng2, w1, w2)


def _mlp1_kernel(x_ref, mod_ref, ng_ref, w1_ref, w2_ref, nf_ref, o_ref, xs):
    d = D_MODEL
    steps = MLP_STEPS
    nb = SUBLANES
    for b in range(nb):
        xs[b * steps:(b + 1) * steps, :] = x_ref[:, b, :]
    x = xs[...]
    xn = _rms(x) * ng_ref[...]
    hs = []
    for b in range(nb):
        mod = mod_ref[b]
        shift, scale = mod[:, 3 * d:4 * d], mod[:, 4 * d:5 * d]
        hs.append((xn[b * steps:(b + 1) * steps] * (1.0 + scale) + shift).astype(BF16))
    acc = _ffn(jnp.concatenate(hs, axis=0), w1_ref, w2_ref)
    for b in range(nb):
        gate = mod_ref[b][:, 5 * d:6 * d]
        y = x[b * steps:(b + 1) * steps] + gate * acc[b * steps:(b + 1) * steps]
        o_ref[b] = _rms(y) * nf_ref[...]


def _mlp1(x_tm, mod, ng, w1, w2, nf):
    seq, bsz, d = x_tm.shape
    steps = MLP_STEPS
    return pl.pallas_call(
        _mlp1_kernel,
        grid=(seq // steps,),
        in_specs=[
            pl.BlockSpec((steps, bsz, d), lambda i: (i, 0, 0)),
            _const_spec((bsz, 1, 6 * d)),
            _const_spec((1, d)),
            _const_spec((d, D_FF)),
            _const_spec((D_FF, d)),
            _const_spec((1, d)),
        ],
        out_specs=pl.BlockSpec((bsz, steps, d), lambda i: (0, i, 0)),
        out_shape=jax.ShapeDtypeStruct((bsz, seq, d), F32),
        scratch_shapes=[pltpu.VMEM((steps * bsz, d), F32)],
        compiler_params=_cparams(("arbitrary",)),
        name="l1_mlp_final",
    )(x_tm, mod, ng, w1, w2, nf)


def _s5_weights(lam_re, lam_im, log_step, b_re, b_im, c_re, c_im):
    lr = jnp.minimum(lam_re.astype(F32), -1e-4)
    li = lam_im.astype(F32)
    dt = jnp.exp(log_step.astype(F32))[..., None]
    mag = jnp.exp(lr * dt)
    ab_re, ab_im = mag * jnp.cos(li * dt), mag * jnp.sin(li * dt)
    den = lr * lr + li * li
    nr, ni = ab_re - 1.0, ab_im
    f_re = (nr * lr + ni * li) / den
    f_im = (ni * lr - nr * li) / den
    br_, bi_ = b_re.astype(F32), b_im.astype(F32)
    bb_re = f_re[..., None] * br_ - f_im[..., None] * bi_
    bb_im = f_re[..., None] * bi_ + f_im[..., None] * br_

    gpk = MXU_K // S5_GROUP
    nkb = S5_GROUPS // gpk
    eye = jnp.eye(gpk, dtype=F32)

    def pack_b(bb):
        bb = bb.reshape(2, nkb, gpk, S5_STATE, S5_GROUP)
        w = jnp.einsum('dkgpc,gh->dkgchp', bb, eye)
        return w.reshape(2, nkb, MXU_K, gpk * S5_STATE)

    def pack_c(cc):
        cc = cc.astype(F32).reshape(2, nkb, gpk, S5_GROUP, S5_STATE)
        w = jnp.einsum('dkgcp,gh->dkgphc', cc, eye)
        return w.reshape(2, nkb, gpk * S5_STATE, MXU_K)

    wb = jnp.concatenate([pack_b(bb_re), pack_b(bb_im)], axis=-1).astype(BF16)
    wc = jnp.concatenate([pack_c(c_re), -pack_c(c_im)], axis=-2).astype(BF16)
    a_re = jnp.broadcast_to(ab_re.reshape(2, 1, S5_NSTATE), (2, SUBLANES, S5_NSTATE))
    a_im = jnp.broadcast_to(ab_im.reshape(2, 1, S5_NSTATE), (2, SUBLANES, S5_NSTATE))
    return a_re, a_im, wb, wc


def _s5_direction(x, mod_ref, ng_ref, are_ref, aim_ref, wb_ref, wc_ref,
                  bur, bui, hre, him, *, reverse):
    d = D_MODEL
    rows = x.shape[0]
    steps = rows // SUBLANES

    @pl.when(pl.program_id(0) == 0)
    def _():
        hre[...] = jnp.zeros_like(hre)
        him[...] = jnp.zeros_like(him)

    mod = mod_ref[...]
    shift, scale = mod[:, 0:d], mod[:, d:2 * d]
    xn = (_rms(x) * ng_ref[...]).reshape(steps, SUBLANES, d)
    u = (xn * (1.0 + scale)[None] + shift[None]).reshape(rows, d)
    ub = u.astype(BF16)

    blk = wb_ref.shape[3] // 2
    nkb = wb_ref.shape[1]

    def project_in(kb):
        r = _dot(ub[:, kb * MXU_K:(kb + 1) * MXU_K], wb_ref[0, kb])
        bur[:, kb * blk:(kb + 1) * blk] = r[:, 0:blk]
        bui[:, kb * blk:(kb + 1) * blk] = r[:, blk:2 * blk]

    def scan(kb):
        for j in range(kb * blk, (kb + 1) * blk, S5_COLS):
            cols = slice(j, j + S5_COLS)
            ar = are_ref[0, :, cols]
            ai = aim_ref[0, :, cols]
            hr = hre[:, cols]
            hi = him[:, cols]
            for s in range(steps):
                tt = (steps - 1 - s) if reverse else s
                rws = slice(tt * SUBLANES, (tt + 1) * SUBLANES)
                nr = ar * hr - ai * hi + bur[rws, cols]
                ni = ar * hi + ai * hr + bui[rws, cols]
                bur[rws, cols] = nr
                bui[rws, cols] = ni
                hr, hi = nr, ni
            hre[:, cols] = hr
            him[:, cols] = hi

    def project_out(kb):
        st = slice(kb * blk, (kb + 1) * blk)
        return (_dot(bur[:, st].astype(BF16), wc_ref[0, kb, 0:blk, :])
                + _dot(bui[:, st].astype(BF16), wc_ref[0, kb, blk:2 * blk, :]))

    ys = [None] * nkb
    project_in(0)
    for kb in range(nkb):
        if kb + 1 < nkb:
            project_in(kb + 1)
        scan(kb)
        ys[kb] = project_out(kb)
    return u, jnp.concatenate(ys, axis=1)


def _s5_fwd_kernel(x_ref, mod_ref, ng_ref, are_ref, aim_ref, wb_ref, wc_ref, yf_ref, xtm_ref,
                   bur, bui, hre, him, xs):
    steps = x_ref.shape[1]
    for b in range(SUBLANES):
        xs[:, b, :] = x_ref[b]
    x = xs[...].reshape(steps * SUBLANES, D_MODEL)
    xtm_ref[...] = x
    _, y = _s5_direction(x, mod_ref, ng_ref, are_ref, aim_ref, wb_ref, wc_ref,
                         bur, bui, hre, him, reverse=False)
    yf_ref[...] = y


def _s5_bwd_kernel(x_ref, mod_ref, ng_ref, are_ref, aim_ref, wb_ref, wc_ref, yf_ref,
                   dsk_ref, wa_ref, wg_ref, o_ref, bur, bui, hre, him):
    d = D_MODEL
    rows = x_ref.shape[0]
    steps = rows // SUBLANES
    x = x_ref[...]
    u, yb = _s5_direction(x, mod_ref, ng_ref, are_ref, aim_ref, wb_ref, wc_ref,
                          bur, bui, hre, him, reverse=True)
    y = yf_ref[...] + yb + dsk_ref[...] * u
    zb = jax.nn.gelu(y).astype(BF16)
    glu = _dot(zb, wa_ref[...]) * jax.nn.sigmoid(_dot(zb, wg_ref[...]))
    gate = mod_ref[...][:, 2 * d:3 * d]
    out = x.reshape(steps, SUBLANES, d) + gate[None] * glu.reshape(steps, SUBLANES, d)
    o_ref[...] = out.reshape(rows, d)


def _s5_layer(x, mod1, ng, a_re, a_im, wb, wc, d_skip, w_glu_a, w_glu_b):
    bsz, seq, d = x.shape
    n = bsz * seq
    rows = S5_STEPS * SUBLANES
    nt = n // rows
    scratch = [
        pltpu.VMEM((rows, S5_NSTATE), F32),
        pltpu.VMEM((rows, S5_NSTATE), F32),
        pltpu.VMEM((SUBLANES, S5_NSTATE), F32),
        pltpu.VMEM((SUBLANES, S5_NSTATE), F32),
    ]

    def dir_specs(direction, x_spec):
        return [
            x_spec,
            _const_spec((SUBLANES, 6 * d)),
            _const_spec((1, d)),
            pl.BlockSpec((1, SUBLANES, S5_NSTATE), lambda i: (direction, 0, 0),
                         pipeline_mode=pl.Buffered(1)),
            pl.BlockSpec((1, SUBLANES, S5_NSTATE), lambda i: (direction, 0, 0),
                         pipeline_mode=pl.Buffered(1)),
            pl.BlockSpec((1,) + wb.shape[1:], lambda i: (direction, 0, 0, 0),
                         pipeline_mode=pl.Buffered(1)),
            pl.BlockSpec((1,) + wc.shape[1:], lambda i: (direction, 0, 0, 0),
                         pipeline_mode=pl.Buffered(1)),
        ]

    bwd = lambda i: nt - 1 - i
    tm_shape = jax.ShapeDtypeStruct((n, d), F32)
    yf, x_tm = pl.pallas_call(
        _s5_fwd_kernel,
        grid=(nt,),
        in_specs=dir_specs(0, pl.BlockSpec((bsz, S5_STEPS, d), lambda i: (0, i, 0))),
        out_specs=[pl.BlockSpec((rows, d), lambda i: (i, 0)),
                   pl.BlockSpec((rows, d), lambda i: (i, 0))],
        out_shape=[tm_shape, tm_shape],
        scratch_shapes=scratch + [pltpu.VMEM((S5_STEPS, SUBLANES, d), F32)],
        compiler_params=_cparams(("arbitrary",)),
        name="s5_fwd",
    )(x, mod1, ng, a_re, a_im, wb, wc)
    return pl.pallas_call(
        _s5_bwd_kernel,
        grid=(nt,),
        in_specs=dir_specs(1, pl.BlockSpec((rows, d), lambda i: (bwd(i), 0))) + [
            pl.BlockSpec((rows, d), lambda i: (bwd(i), 0)),
            _const_spec((1, d)),
            _const_spec((d, d)),
            _const_spec((d, d)),
        ],
        out_specs=pl.BlockSpec((rows, d), lambda i: (bwd(i), 0)),
        out_shape=jax.ShapeDtypeStruct((n, d), F32),
        scratch_shapes=scratch,
        compiler_params=_cparams(("arbitrary",)),
        name="s5_bwd",
    )(x_tm, mod1, ng, a_re, a_im, wb, wc, yf, d_skip, w_glu_a, w_glu_b)


def kernel(x, c, norm_g, ada_w, ada_b, w_in, conv_w, conv_b, cln_g, cln_b, w_out, s5_lam_re,
           s5_lam_im, s5_log_step, s5_b_re, s5_b_im, s5_c_re, s5_c_im, s5_d, w_glu_a, w_glu_b,
           w_fc1, w_fc2, norm_f):
    bsz, seq, d = x.shape
    assert d == D_MODEL and bsz == SUBLANES
    assert seq % T_RET == 0 and seq % MLP_STEPS == 0 and seq % S5_STEPS == 0
    assert norm_g.shape[0] == 2, "one conv/retention layer followed by one S5 layer"

    mod = _modulation(c, ada_w, ada_b)
    mod0 = mod[0].reshape(bsz, 1, 6 * d)
    mod1 = mod[1]
    row = lambda v: v.reshape(1, -1).astype(F32)

    dmat, w_kf, w_kb, w_qf, w_qb, g_tile = _retention_tables(T_RET)
    cosf, sinf = _rotary_tables(seq)
    a, q, k, v, g, kvf, kvb = _inproj(x, mod0, row(norm_g[0, 0]), w_in[0].astype(BF16),
                                      cosf, sinf, w_kf, w_kb)
    states = _retstate(g_tile, kvf, kvb)
    cw = jnp.concatenate([conv_w[0], jnp.zeros((1, MIX_A), conv_w.dtype)], axis=0)
    x = _mixer0(x, mod0, a, q, k, v, g, states, dmat, w_qf, w_qb, cw, row(conv_b[0]),
                row(cln_g[0]), row(cln_b[0]), w_out[0].astype(BF16), row(norm_g[0, 1]),
                w_fc1[0].astype(BF16), w_fc2[0].astype(BF16))

    a_re, a_im, wb, wc = _s5_weights(s5_lam_re[0], s5_lam_im[0], s5_log_step[0], s5_b_re[0],
                                     s5_b_im[0], s5_c_re[0], s5_c_im[0])
    x = _s5_layer(x, mod1, row(norm_g[1, 0]), a_re, a_im, wb, wc,
                  row(s5_d[0]), w_glu_a[0].astype(BF16), w_glu_b[0].astype(BF16))
    return _mlp1(x.reshape(seq, bsz, d), mod1.reshape(bsz, 1, 6 * d), row(norm_g[1, 1]),
                 w_fc1[1].astype(BF16), w_fc2[1].astype(BF16), row(norm_f))
```

```python
import numpy as np
import jax
import jax.numpy as jnp
from jax import lax
from jax.experimental import pallas as pl
from jax.experimental.pallas import tpu as pltpu

F32 = jnp.float32
BF16 = jnp.bfloat16

D_MODEL = 1024
MIX_A = D_MODEL // 2
RET_HEADS = 4
RET_HEAD_DIM = 128
RET_WIDTH = RET_HEADS * RET_HEAD_DIM
IN_COLS = 2 * MIX_A + 4 * RET_WIDTH
CONV_WIDTH = 31
CONV_HALO = 16
ROPE_BASE = 10000.0
S5_GROUP = 16
S5_GROUPS = D_MODEL // S5_GROUP
S5_STATE = 64
S5_NSTATE = S5_GROUPS * S5_STATE
D_FF = 4 * D_MODEL
EPS = 1e-6

SUBLANES = 8
LANES = 128
T_RET = 512
MLP_STEPS = 64
S5_STEPS = 64
S5_SUBTILES = 2
S5_COLS = 512
CONV_ROWS = 32
CONV_COPY_ROWS = 64
VMEM_LIMIT = 56 * 1024 * 1024


def _cparams(sem):
    return pltpu.CompilerParams(dimension_semantics=sem, vmem_limit_bytes=VMEM_LIMIT)


def _const_spec(shape):
    nd = len(shape)
    return pl.BlockSpec(shape, lambda *_: (0,) * nd, pipeline_mode=pl.Buffered(1))


def _rms(x):
    return x * lax.rsqrt(jnp.mean(x * x, axis=-1, keepdims=True) + EPS)


def _dot(a, b):
    return jnp.dot(a, b, preferred_element_type=F32)


def _ffn(hb, w1_ref, w2_ref):
    d = D_MODEL
    acc = jnp.zeros((hb.shape[0], d), F32)
    for j in range(0, D_FF, d):
        u = jnp.maximum(_dot(hb, w1_ref[:, j:j + d]), 0.0)
        acc = acc + _dot((u * u).astype(BF16), w2_ref[j:j + d, :])
    return acc


def _mod_kernel(c_ref, w_ref, b_ref, o_ref):
    c = c_ref[...]
    ca = c * jax.nn.sigmoid(c)
    w = w_ref[0]
    ch = ca.astype(BF16)
    cl = (ca - ch.astype(F32)).astype(BF16)
    wh = w.astype(BF16)
    wl = (w - wh.astype(F32)).astype(BF16)
    o_ref[0] = _dot(ch, wh) + _dot(cl, wh) + _dot(ch, wl) + b_ref[0]


def _modulation(c, ada_w, ada_b):
    depth, d, d6 = ada_w.shape
    bsz = c.shape[0]
    nblk = d6 // d
    return pl.pallas_call(
        _mod_kernel,
        grid=(depth, nblk),
        in_specs=[
            pl.BlockSpec((bsz, d), lambda l, j: (0, 0)),
            pl.BlockSpec((1, d, d), lambda l, j: (l, 0, j)),
            pl.BlockSpec((1, 1, d), lambda l, j: (l, 0, j)),
        ],
        out_specs=pl.BlockSpec((1, bsz, d), lambda l, j: (l, 0, j)),
        out_shape=jax.ShapeDtypeStruct((depth, bsz, d6), F32),
        compiler_params=_cparams(("arbitrary", "arbitrary")),
        name="adaln_mod",
    )(c, ada_w, ada_b.reshape(depth, 1, d6))


def _retention_tables(tile):
    h = np.arange(RET_HEADS, dtype=np.float64)
    log_g = np.log1p(-np.exp2(-5.0 - h))
    idx = np.arange(tile, dtype=np.float64)
    dist = np.abs(idx[:, None] - idx[None, :])
    dmat = np.exp(log_g[:, None, None] * dist)

    def lanes(w):
        return np.broadcast_to(w[:, :, None], (RET_HEADS, tile, RET_HEAD_DIM))

    w_kf = lanes(np.exp(log_g[:, None] * (tile - 1.0 - idx)[None, :]))
    w_kb = lanes(np.exp(log_g[:, None] * idx[None, :]))
    w_qf = lanes(np.exp(log_g[:, None] * (idx + 1.0)[None, :]))
    w_qb = lanes(np.exp(log_g[:, None] * (tile - idx)[None, :]))
    g_tile = np.exp(log_g * tile)
    f = lambda a: jnp.asarray(np.ascontiguousarray(a), dtype=F32)
    return f(dmat), f(w_kf), f(w_kb), f(w_qf), f(w_qb), f(g_tile)


def _rotary_tables(seq):
    half = RET_HEAD_DIM // 2
    inv = ROPE_BASE ** (-np.arange(0, RET_HEAD_DIM, 2, dtype=np.float64) / RET_HEAD_DIM)
    ang = np.arange(seq, dtype=np.float64)[:, None] * inv[None, :]
    cos, sin = np.cos(ang), np.sin(ang)
    cosf = np.concatenate([cos, cos], axis=1)
    sinf = np.concatenate([-sin, sin], axis=1)
    assert cosf.shape == (seq, 2 * half)
    return jnp.asarray(cosf, dtype=F32), jnp.asarray(sinf, dtype=F32)


def _inproj_kernel(x_ref, mod_ref, ng_ref, win_ref, cos_ref, sin_ref, wkf_ref, wkb_ref,
                   a_ref, q_ref, k_ref, v_ref, g_ref, kvf_ref, kvb_ref):
    d = D_MODEL
    x = x_ref[0]
    mod = mod_ref[0]
    shift, scale = mod[:, 0:d], mod[:, d:2 * d]
    h = _rms(x) * ng_ref[...] * (1.0 + scale) + shift
    hb = h.astype(BF16)

    a_val = _dot(hb, win_ref[:, 0:MIX_A])
    a_gate = _dot(hb, win_ref[:, MIX_A:2 * MIX_A])
    a_ref[0] = a_val * jax.nn.sigmoid(a_gate)

    cosf = cos_ref[...]
    sinf = sin_ref[...]
    half = RET_HEAD_DIM // 2
    base = 2 * MIX_A
    g_ref[0] = _dot(hb, win_ref[:, base + 3 * RET_WIDTH:base + 4 * RET_WIDTH]).astype(BF16)
    v_all = _dot(hb, win_ref[:, base + 2 * RET_WIDTH:base + 3 * RET_WIDTH]).astype(BF16)
    v_ref[0] = v_all
    q_all = _dot(hb, win_ref[:, base:base + RET_WIDTH])
    k_all = _dot(hb, win_ref[:, base + RET_WIDTH:base + 2 * RET_WIDTH])
    for hd in range(RET_HEADS):
        lo = hd * RET_HEAD_DIM
        hi = lo + RET_HEAD_DIM
        qh = q_all[:, lo:hi]
        kh = k_all[:, lo:hi]
        qh = (qh * cosf + pltpu.roll(qh, half, 1) * sinf) * (RET_HEAD_DIM ** -0.5)
        kh = kh * cosf + pltpu.roll(kh, half, 1) * sinf
        vb = v_all[:, lo:hi]
        q_ref[0, :, lo:hi] = qh.astype(BF16)
        k_ref[0, :, lo:hi] = kh.astype(BF16)
        tn = (((0,), (0,)), ((), ()))
        kvf_ref[0, 0, hd] = lax.dot_general((kh * wkf_ref[hd]).astype(BF16), vb, tn,
                                            preferred_element_type=F32)
        kvb_ref[0, 0, hd] = lax.dot_general((kh * wkb_ref[hd]).astype(BF16), vb, tn,
                                            preferred_element_type=F32)


def _inproj(x, mod0, ng, w_in, cosf, sinf, w_kf, w_kb):
    bsz, seq, d = x.shape
    t = T_RET
    nt = seq // t
    hd = RET_HEAD_DIM
    row = lambda b, i: (b, i, 0)
    kv_spec = pl.BlockSpec((1, 1, RET_HEADS, hd, hd), lambda b, i: (b, i, 0, 0, 0))
    kv_shape = jax.ShapeDtypeStruct((bsz, nt, RET_HEADS, hd, hd), F32)
    return pl.pallas_call(
        _inproj_kernel,
        grid=(bsz, nt),
        in_specs=[
            pl.BlockSpec((1, t, d), row),
            pl.BlockSpec((1, 1, 6 * d), lambda b, i: (b, 0, 0)),
            _const_spec((1, d)),
            _const_spec((d, IN_COLS)),
            pl.BlockSpec((t, hd), lambda b, i: (i, 0)),
            pl.BlockSpec((t, hd), lambda b, i: (i, 0)),
            _const_spec((RET_HEADS, t, hd)),
            _const_spec((RET_HEADS, t, hd)),
        ],
        out_specs=[
            pl.BlockSpec((1, t, MIX_A), row),
            pl.BlockSpec((1, t, RET_WIDTH), row),
            pl.BlockSpec((1, t, RET_WIDTH), row),
            pl.BlockSpec((1, t, RET_WIDTH), row),
            pl.BlockSpec((1, t, RET_WIDTH), row),
            kv_spec, kv_spec,
        ],
        out_shape=[
            jax.ShapeDtypeStruct((bsz, seq, MIX_A), F32),
            jax.ShapeDtypeStruct((bsz, seq, RET_WIDTH), BF16),
            jax.ShapeDtypeStruct((bsz, seq, RET_WIDTH), BF16),
            jax.ShapeDtypeStruct((bsz, seq, RET_WIDTH), BF16),
            jax.ShapeDtypeStruct((bsz, seq, RET_WIDTH), BF16),
            kv_shape, kv_shape,
        ],
        compiler_params=_cparams(("arbitrary", "arbitrary")),
        name="l0_inproj",
    )(x, mod0, ng, w_in, cosf, sinf, w_kf, w_kb)


def _retstate_kernel(gl_ref, kvf_ref, kvb_ref, s_ref):
    g = gl_ref[pl.program_id(1)]
    nt = kvf_ref.shape[1]
    hd = RET_HEAD_DIM
    s = jnp.zeros((hd, hd), F32)
    for c in range(nt):
        s_ref[0, c, 0, :, 0:hd] = s.astype(BF16)
        s = g * s + kvf_ref[0, c, 0]
    s = jnp.zeros((hd, hd), F32)
    for c in reversed(range(nt)):
        s_ref[0, c, 0, :, hd:2 * hd] = s.astype(BF16)
        s = g * s + kvb_ref[0, c, 0]


def _retstate(g_tile, kvf, kvb):
    bsz, nt, nh, hd, _ = kvf.shape
    kv_spec = pl.BlockSpec((1, nt, 1, hd, hd), lambda b, h: (b, 0, h, 0, 0))
    return pl.pallas_call(
        _retstate_kernel,
        grid=(bsz, nh),
        in_specs=[pl.BlockSpec(memory_space=pltpu.SMEM), kv_spec, kv_spec],
        out_specs=pl.BlockSpec((1, nt, 1, hd, 2 * hd), lambda b, h: (b, 0, h, 0, 0)),
        out_shape=jax.ShapeDtypeStruct((bsz, nt, nh, hd, 2 * hd), BF16),
        compiler_params=_cparams(("arbitrary", "arbitrary")),
        name="l0_retstate",
    )(g_tile, kvf, kvb)


def _mixer0_kernel(x_ref, mod_ref, a_ref, ap_ref, an_ref, q_ref, k_ref, v_ref, g_ref,
                   s_ref, dmat_ref, wqf_ref, wqb_ref, cw_ref, cb_ref, lg_ref, lb_ref, wout_ref,
                   ng2_ref, w1_ref, w2_ref, o_ref, aext, ash, cat):
    d = D_MODEL
    t = a_ref.shape[1]
    i = pl.program_id(1)
    last = pl.num_programs(1) - 1
    halo = CONV_HALO

    aext[0:halo, :] = jnp.where(i > 0, ap_ref[0], 0.0)
    aext[halo:halo + t, :] = a_ref[0]
    aext[halo + t:halo + t + halo, :] = jnp.where(i < last, an_ref[0], 0.0)

    off = halo - CONV_WIDTH // 2
    span = t + SUBLANES * ((off + CONV_WIDTH - 1) // SUBLANES)
    for s in range(1, SUBLANES):
        for r0 in range(0, span, CONV_COPY_ROWS):
            n = min(CONV_COPY_ROWS, span - r0)
            ash[s - 1, r0:r0 + n, :] = aext[r0 + s:r0 + s + n, :]

    cb = cb_ref[...]
    lg = lg_ref[...]
    lb = lb_ref[...]
    for r0 in range(0, t, CONV_ROWS):
        acc = jnp.zeros((CONV_ROWS, MIX_A), F32)
        for kk in range(CONV_WIDTH):
            s = (off + kk) % SUBLANES
            base = r0 + (off + kk) - s
            if s == 0:
                tap = aext[base:base + CONV_ROWS, :]
            else:
                tap = ash[s - 1, base:base + CONV_ROWS, :]
            acc = acc + tap * cw_ref[kk:kk + 1, :]
        acc = acc + cb
        mu = jnp.mean(acc, axis=-1, keepdims=True)
        xc = acc - mu
        var = jnp.mean(xc * xc, axis=-1, keepdims=True)
        y = xc * lax.rsqrt(var + EPS) * lg + lb
        cat[r0:r0 + CONV_ROWS, 0:MIX_A] = (y * jax.nn.sigmoid(y)).astype(BF16)

    hd = RET_HEAD_DIM
    nt_dims = (((1,), (1,)), ((), ()))
    for h in range(RET_HEADS):
        lo = h * hd
        qh = q_ref[0, :, lo:lo + hd]
        kh = k_ref[0, :, lo:lo + hd]
        vh = v_ref[0, :, lo:lo + hd]
        sc = lax.dot_general(qh, kh, nt_dims, preferred_element_type=F32)
        p = (sc * dmat_ref[h]).astype(BF16)
        r = _dot(p, vh)
        cross = _dot(qh, s_ref[0, 0, h])
        r = r + wqf_ref[h] * cross[:, 0:hd] + wqb_ref[h] * cross[:, hd:2 * hd]
        r = _rms(r)
        gh = g_ref[0, :, lo:lo + hd].astype(F32)
        cat[:, MIX_A + lo:MIX_A + lo + hd] = (gh * jax.nn.sigmoid(gh) * r).astype(BF16)

    mod = mod_ref[0]
    x1 = x_ref[0] + mod[:, 2 * d:3 * d] * _dot(cat[...], wout_ref[...])

    shift, scale, gate = mod[:, 3 * d:4 * d], mod[:, 4 * d:5 * d], mod[:, 5 * d:6 * d]
    hb = (_rms(x1) * ng2_ref[...] * (1.0 + scale) + shift).astype(BF16)
    o_ref[0] = x1 + gate * _ffn(hb, w1_ref, w2_ref)


def _mixer0(x, mod0, a, q, k, v, g, states, dmat, w_qf, w_qb, conv_w, conv_b, cln_g, cln_b,
            w_out, ng2, w1, w2):
    bsz, seq, d = x.shape
    t = T_RET
    nt = seq // t
    hd = RET_HEAD_DIM
    hb = t // CONV_HALO
    nhalo = seq // CONV_HALO
    row = lambda b, i: (b, i, 0)
    return pl.pallas_call(
        _mixer0_kernel,
        grid=(bsz, nt),
        in_specs=[
            pl.BlockSpec((1, t, d), row),
            pl.BlockSpec((1, 1, 6 * d), lambda b, i: (b, 0, 0)),
            pl.BlockSpec((1, t, MIX_A), row),
            pl.BlockSpec((1, CONV_HALO, MIX_A), lambda b, i: (b, jnp.maximum(i * hb - 1, 0), 0)),
            pl.BlockSpec((1, CONV_HALO, MIX_A),
                         lambda b, i: (b, jnp.minimum((i + 1) * hb, nhalo - 1), 0)),
            pl.BlockSpec((1, t, RET_WIDTH), row),
            pl.BlockSpec((1, t, RET_WIDTH), row),
            pl.BlockSpec((1, t, RET_WIDTH), row),
            pl.BlockSpec((1, t, RET_WIDTH), row),
            pl.BlockSpec((1, 1, RET_HEADS, hd, 2 * hd), lambda b, i: (b, i, 0, 0, 0)),
            _const_spec((RET_HEADS, t, t)),
            _const_spec((RET_HEADS, t, hd)),
            _const_spec((RET_HEADS, t, hd)),
            _const_spec((CONV_WIDTH + 1, MIX_A)),
            _const_spec((1, MIX_A)),
            _const_spec((1, MIX_A)),
            _const_spec((1, MIX_A)),
            _const_spec((d, d)),
            _const_spec((1, d)),
            _const_spec((d, D_FF)),
            _const_spec((D_FF, d)),
        ],
        out_specs=pl.BlockSpec((1, t, d), row),
        out_shape=jax.ShapeDtypeStruct((bsz, seq, d), F32),
        scratch_shapes=[
            pltpu.VMEM((t + 2 * CONV_HALO, MIX_A), F32),
            pltpu.VMEM((SUBLANES - 1, t + 2 * CONV_HALO, MIX_A), F32),
            pltpu.VMEM((t, d), BF16),
        ],
        compiler_params=_cparams(("arbitrary", "arbitrary")),
        name="l0_mixer_mlp",
    )(x, mod0, a, a, a, q, k, v, g, states, dmat, w_qf, w_qb, conv_w, conv_b, cln_g, cln_b,
      w_out, ng2, w1, w2)


def _mlp1_kernel(x_ref, mod_ref, ng_ref, w1_ref, w2_ref, nf_ref, o_ref, xs):
    d = D_MODEL
    steps = MLP_STEPS
    nb = SUBLANES
    for b in range(nb):
        xs[b * steps:(b + 1) * steps, :] = x_ref[:, b, :]
    x = xs[...]
    xn = _rms(x) * ng_ref[...]
    hs = []
    for b in range(nb):
        mod = mod_ref[b]
        shift, scale = mod[:, 3 * d:4 * d], mod[:, 4 * d:5 * d]
        hs.append((xn[b * steps:(b + 1) * steps] * (1.0 + scale) + shift).astype(BF16))
    acc = _ffn(jnp.concatenate(hs, axis=0), w1_ref, w2_ref)
    for b in range(nb):
        gate = mod_ref[b][:, 5 * d:6 * d]
        y = x[b * steps:(b + 1) * steps] + gate * acc[b * steps:(b + 1) * steps]
        o_ref[b] = _rms(y) * nf_ref[...]


def _mlp1(x_tm, mod, ng, w1, w2, nf):
    seq, bsz, d = x_tm.shape
    steps = MLP_STEPS
    return pl.pallas_call(
        _mlp1_kernel,
        grid=(seq // steps,),
        in_specs=[
            pl.BlockSpec((steps, bsz, d), lambda i: (i, 0, 0)),
            _const_spec((bsz, 1, 6 * d)),
            _const_spec((1, d)),
            _const_spec((d, D_FF)),
            _const_spec((D_FF, d)),
            _const_spec((1, d)),
        ],
        out_specs=pl.BlockSpec((bsz, steps, d), lambda i: (0, i, 0)),
        out_shape=jax.ShapeDtypeStruct((bsz, seq, d), F32),
        scratch_shapes=[pltpu.VMEM((steps * bsz, d), F32)],
        compiler_params=_cparams(("arbitrary",)),
        name="l1_mlp_final",
    )(x_tm, mod, ng, w1, w2, nf)


def _s5_weights(lam_re, lam_im, log_step, b_re, b_im, c_re, c_im):
    lr = jnp.minimum(lam_re.astype(F32), -1e-4)
    li = lam_im.astype(F32)
    dt = jnp.exp(log_step.astype(F32))[..., None]
    mag = jnp.exp(lr * dt)
    ab_re, ab_im = mag * jnp.cos(li * dt), mag * jnp.sin(li * dt)
    den = lr * lr + li * li
    nr, ni = ab_re - 1.0, ab_im
    f_re = (nr * lr + ni * li) / den
    f_im = (ni * lr - nr * li) / den
    br_, bi_ = b_re.astype(F32), b_im.astype(F32)
    bb_re = f_re[..., None] * br_ - f_im[..., None] * bi_
    bb_im = f_re[..., None] * bi_ + f_im[..., None] * br_

    cr, ci = c_re.astype(F32), c_im.astype(F32)

    def cmul(xr, xi, yr, yi):
        return xr * yr - xi * yi, xr * yi + xi * yr

    a2_re, a2_im = cmul(ab_re, ab_im, ab_re, ab_im)
    abb_re, abb_im = cmul(ab_re[..., None], ab_im[..., None], bb_re, bb_im)
    ca_re, ca_im = cmul(cr, ci, ab_re[:, :, None, :], ab_im[:, :, None, :])
    ca2_re, ca2_im = cmul(cr, ci, a2_re[:, :, None, :], a2_im[:, :, None, :])

    def re_cb(br, bi):
        hp = lax.Precision.HIGHEST
        return (jnp.einsum('dgcp,dgpk->dgck', cr, br, precision=hp)
                - jnp.einsum('dgcp,dgpk->dgck', ci, bi, precision=hp))

    k0 = re_cb(bb_re, bb_im)
    k1 = re_cb(abb_re, abb_im)

    gpo = LANES // S5_GROUP
    noct = S5_GROUPS // gpo
    eye = jnp.eye(gpo, dtype=F32)

    def pack_b(bb):
        bb = bb.reshape(2, noct, gpo, S5_STATE, S5_GROUP)
        return jnp.einsum('dogpc,gh->dogchp', bb, eye).reshape(2, noct, LANES, gpo * S5_STATE)

    def pack_c(cc):
        cc = cc.reshape(2, noct, gpo, S5_GROUP, S5_STATE)
        return jnp.einsum('dogcp,gh->dogphc', cc, eye).reshape(2, noct, gpo * S5_STATE, LANES)

    def pack_k(kk):
        kk = kk.reshape(2, noct, gpo, S5_GROUP, S5_GROUP)
        return jnp.einsum('dogck,gh->dogkhc', kk, eye).reshape(2, noct, LANES, LANES)

    cat = jnp.concatenate
    wb = cat([cat([pack_b(abb_re), pack_b(abb_im)], axis=-1),
              cat([pack_b(bb_re), pack_b(bb_im)], axis=-1)], axis=-2).astype(BF16)
    wch = cat([cat([pack_c(ca_re), pack_c(ca2_re)], axis=-1),
               cat([-pack_c(ca_im), -pack_c(ca2_im)], axis=-1)], axis=-2).astype(BF16)
    wcu = cat([cat([pack_k(k0), pack_k(k1)], axis=-1),
               cat([jnp.zeros((2, noct, LANES, LANES), F32), pack_k(k0)], axis=-1)],
              axis=-2).astype(BF16)
    a2r = jnp.broadcast_to(a2_re.reshape(2, 1, S5_NSTATE), (2, SUBLANES, S5_NSTATE))
    a2i = jnp.broadcast_to(a2_im.reshape(2, 1, S5_NSTATE), (2, SUBLANES, S5_NSTATE))
    return a2r, a2i, wb, wch, wcu


def _s5_direction(x_pair, mod_ref, ng_ref, a2r_ref, a2i_ref, wb_ref, wch_ref, wcu_ref,
                  hr_buf, hi_buf, *, reverse):
    d = D_MODEL
    rows = x_pair[0].shape[0]
    pairs = rows // SUBLANES
    first, second = (1, 0) if reverse else (0, 1)

    mod = mod_ref[...]
    shift, scale = mod[:, 0:d], mod[:, d:2 * d]

    def modulate(x):
        xn = (_rms(x) * ng_ref[...]).reshape(pairs, SUBLANES, d)
        return (xn * (1.0 + scale)[None] + shift[None]).reshape(rows, d)

    u_pair = [modulate(x_pair[0]), modulate(x_pair[1])]
    ub = [u.astype(BF16) for u in u_pair]

    noct = wb_ref.shape[1]
    ost = wb_ref.shape[3] // 2
    if reverse:
        own, prev, carry_src, carry_dst = slice(0, rows), slice(SUBLANES, rows + SUBLANES), \
            slice(0, SUBLANES), slice(rows, rows + SUBLANES)
    else:
        own, prev, carry_src, carry_dst = slice(SUBLANES, rows + SUBLANES), slice(0, rows), \
            slice(rows, rows + SUBLANES), slice(0, SUBLANES)
    hr_buf[carry_dst, :] = hr_buf[carry_src, :]
    hi_buf[carry_dst, :] = hi_buf[carry_src, :]

    def lhs_u(o):
        lanes = slice(o * LANES, (o + 1) * LANES)
        return jnp.concatenate([ub[first][:, lanes], ub[second][:, lanes]], axis=1)

    def project_in(o):
        r = _dot(lhs_u(o), wb_ref[0, o])
        hr_buf[own, o * ost:(o + 1) * ost] = r[:, 0:ost]
        hi_buf[own, o * ost:(o + 1) * ost] = r[:, ost:2 * ost]

    def scan(o):
        for j in range(o * ost, (o + 1) * ost, S5_COLS):
            cols = slice(j, j + S5_COLS)
            ar = a2r_ref[0, :, cols]
            ai = a2i_ref[0, :, cols]
            hr = hr_buf[carry_dst, cols]
            hi = hi_buf[carry_dst, cols]
            for s in range(pairs):
                p = (pairs - 1 - s) if reverse else s
                rws = slice(own.start + p * SUBLANES, own.start + (p + 1) * SUBLANES)
                nr = ar * hr - ai * hi + hr_buf[rws, cols]
                ni = ar * hi + ai * hr + hi_buf[rws, cols]
                hr_buf[rws, cols] = nr
                hi_buf[rws, cols] = ni
                hr, hi = nr, ni

    def project_out(o):
        st = slice(o * ost, (o + 1) * ost)
        return (_dot(hr_buf[prev, st].astype(BF16), wch_ref[0, o, 0:ost, :])
                + _dot(hi_buf[prev, st].astype(BF16), wch_ref[0, o, ost:2 * ost, :])
                + _dot(lhs_u(o), wcu_ref[0, o]))

    outs = [None] * noct
    project_in(0)
    for o in range(noct):
        if o + 1 < noct:
            project_in(o + 1)
        scan(o)
        outs[o] = project_out(o)
    y_pair = [None, None]
    y_pair[first] = jnp.concatenate([r[:, 0:LANES] for r in outs], axis=1)
    y_pair[second] = jnp.concatenate([r[:, LANES:2 * LANES] for r in outs], axis=1)
    return u_pair, y_pair


def _s5_init_state(hr_buf, hi_buf):
    @pl.when(pl.program_id(0) == 0)
    def _():
        hr_buf[...] = jnp.zeros_like(hr_buf)
        hi_buf[...] = jnp.zeros_like(hi_buf)


def _s5_fwd_kernel(x_ref, mod_ref, ng_ref, a2r_ref, a2i_ref, wb_ref, wch_ref, wcu_ref,
                   yf_ref, xtm_ref, hr_buf, hi_buf, xs):
    d = D_MODEL
    pairs = S5_STEPS // 2
    rows = pairs * SUBLANES
    _s5_init_state(hr_buf, hi_buf)
    for b in range(SUBLANES):
        xs[:, b, :] = x_ref[b]
    for sub in range(S5_SUBTILES):
        x4 = xs[sub * S5_STEPS:(sub + 1) * S5_STEPS].reshape(pairs, 2, SUBLANES, d)
        xtm_ref[sub * pairs:(sub + 1) * pairs] = x4
        x_pair = [x4[:, ph].reshape(rows, d) for ph in range(2)]
        _, y_pair = _s5_direction(x_pair, mod_ref, ng_ref, a2r_ref, a2i_ref, wb_ref, wch_ref,
                                  wcu_ref, hr_buf, hi_buf, reverse=False)
        for ph in range(2):
            yf_ref[sub * pairs:(sub + 1) * pairs, ph] = y_pair[ph].reshape(pairs, SUBLANES, d)


def _s5_bwd_kernel(x_ref, mod_ref, ng_ref, a2r_ref, a2i_ref, wb_ref, wch_ref, wcu_ref, yf_ref,
                   dsk_ref, wa_ref, wg_ref, o_ref, hr_buf, hi_buf):
    d = D_MODEL
    pairs = S5_STEPS // 2
    rows = pairs * SUBLANES
    _s5_init_state(hr_buf, hi_buf)
    gate = mod_ref[...][:, 2 * d:3 * d]
    for sub in reversed(range(S5_SUBTILES)):
        prs = slice(sub * pairs, (sub + 1) * pairs)
        x_pair = [x_ref[prs, ph].reshape(rows, d) for ph in range(2)]
        u_pair, yb_pair = _s5_direction(x_pair, mod_ref, ng_ref, a2r_ref, a2i_ref, wb_ref,
                                        wch_ref, wcu_ref, hr_buf, hi_buf, reverse=True)
        zs = []
        for ph in range(2):
            y = yf_ref[prs, ph].reshape(rows, d) + yb_pair[ph] + dsk_ref[...] * u_pair[ph]
            zs.append(jax.nn.gelu(y).astype(BF16))
        zb = jnp.concatenate(zs, axis=0)
        glu = _dot(zb, wa_ref[...]) * jax.nn.sigmoid(_dot(zb, wg_ref[...]))
        for ph in range(2):
            g = glu[ph * rows:(ph + 1) * rows].reshape(pairs, SUBLANES, d)
            o_ref[prs, ph] = x_pair[ph].reshape(pairs, SUBLANES, d) + gate[None] * g


def _s5_layer(x, mod1, ng, a2r, a2i, wb, wch, wcu, d_skip, w_glu_a, w_glu_b):
    bsz, seq, d = x.shape
    sub_rows = S5_STEPS // 2 * SUBLANES
    tile_steps = S5_STEPS * S5_SUBTILES
    nt = seq // tile_steps
    scratch = [
        pltpu.VMEM((sub_rows + SUBLANES, S5_NSTATE), F32),
        pltpu.VMEM((sub_rows + SUBLANES, S5_NSTATE), F32),
    ]

    def dir_spec(w):
        nd = w.ndim - 1
        return lambda direction: pl.BlockSpec((1,) + w.shape[1:],
                                              lambda i: (direction,) + (0,) * nd,
                                              pipeline_mode=pl.Buffered(1))

    def dir_specs(direction, x_spec):
        return [x_spec, _const_spec((SUBLANES, 6 * d)), _const_spec((1, d))] + [
            dir_spec(w)(direction) for w in (a2r, a2i, wb, wch, wcu)]

    bwd = lambda i: nt - 1 - i
    tm_block = (tile_steps // 2, 2, bsz, d)
    tm_shape = jax.ShapeDtypeStruct((seq // 2, 2, bsz, d), F32)
    yf, x_tm = pl.pallas_call(
        _s5_fwd_kernel,
        grid=(nt,),
        in_specs=dir_specs(0, pl.BlockSpec((bsz, tile_steps, d), lambda i: (0, i, 0))),
        out_specs=[pl.BlockSpec(tm_block, lambda i: (i, 0, 0, 0)),
                   pl.BlockSpec(tm_block, lambda i: (i, 0, 0, 0))],
        out_shape=[tm_shape, tm_shape],
        scratch_shapes=scratch + [pltpu.VMEM((tile_steps, SUBLANES, d), F32)],
        compiler_params=_cparams(("arbitrary",)),
        name="s5_fwd",
    )(x, mod1, ng, a2r, a2i, wb, wch, wcu)
    return pl.pallas_call(
        _s5_bwd_kernel,
        grid=(nt,),
        in_specs=dir_specs(1, pl.BlockSpec(tm_block, lambda i: (bwd(i), 0, 0, 0))) + [
            pl.BlockSpec(tm_block, lambda i: (bwd(i), 0, 0, 0)),
            _const_spec((1, d)),
            _const_spec((d, d)),
            _const_spec((d, d)),
        ],
        out_specs=pl.BlockSpec(tm_block, lambda i: (bwd(i), 0, 0, 0)),
        out_shape=tm_shape,
        scratch_shapes=scratch,
        compiler_params=_cparams(("arbitrary",)),
        name="s5_bwd",
    )(x_tm, mod1, ng, a2r, a2i, wb, wch, wcu, yf, d_skip, w_glu_a, w_glu_b)


def kernel(x, c, norm_g, ada_w, ada_b, w_in, conv_w, conv_b, cln_g, cln_b, w_out, s5_lam_re,
           s5_lam_im, s5_log_step, s5_b_re, s5_b_im, s5_c_re, s5_c_im, s5_d, w_glu_a, w_glu_b,
           w_fc1, w_fc2, norm_f):
    bsz, seq, d = x.shape
    assert d == D_MODEL and bsz == SUBLANES
    assert seq % T_RET == 0 and seq % MLP_STEPS == 0 and seq % (S5_STEPS * S5_SUBTILES) == 0
    assert norm_g.shape[0] == 2, "one conv/retention layer followed by one S5 layer"

    mod = _modulation(c, ada_w, ada_b)
    mod0 = mod[0].reshape(bsz, 1, 6 * d)
    mod1 = mod[1]
    row = lambda v: v.reshape(1, -1).astype(F32)

    dmat, w_kf, w_kb, w_qf, w_qb, g_tile = _retention_tables(T_RET)
    cosf, sinf = _rotary_tables(seq)
    a, q, k, v, g, kvf, kvb = _inproj(x, mod0, row(norm_g[0, 0]), w_in[0].astype(BF16),
                                      cosf, sinf, w_kf, w_kb)
    states = _retstate(g_tile, kvf, kvb)
    cw = jnp.concatenate([conv_w[0], jnp.zeros((1, MIX_A), conv_w.dtype)], axis=0)
    x = _mixer0(x, mod0, a, q, k, v, g, states, dmat, w_qf, w_qb, cw, row(conv_b[0]),
                row(cln_g[0]), row(cln_b[0]), w_out[0].astype(BF16), row(norm_g[0, 1]),
                w_fc1[0].astype(BF16), w_fc2[0].astype(BF16))

    s5w = _s5_weights(s5_lam_re[0], s5_lam_im[0], s5_log_step[0], s5_b_re[0],
                      s5_b_im[0], s5_c_re[0], s5_c_im[0])
    x = _s5_layer(x, mod1, row(norm_g[1, 0]), *s5w,
                  row(s5_d[0]), w_glu_a[0].astype(BF16), w_glu_b[0].astype(BF16))
    return _mlp1(x.reshape(seq, bsz, d), mod1.reshape(bsz, 1, 6 * d), row(norm_g[1, 1]),
                 w_fc1[1].astype(BF16), w_fc2[1].astype(BF16), row(norm_f))
```

```python
import numpy as np
import jax
import jax.numpy as jnp
from jax import lax
from jax.experimental import pallas as pl
from jax.experimental.pallas import tpu as pltpu

F32 = jnp.float32
BF16 = jnp.bfloat16

D_MODEL = 1024
MIX_A = D_MODEL // 2
RET_HEADS = 4
RET_HEAD_DIM = 128
RET_WIDTH = RET_HEADS * RET_HEAD_DIM
IN_COLS = 2 * MIX_A + 4 * RET_WIDTH
CONV_WIDTH = 31
CONV_HALO = 16
ROPE_BASE = 10000.0
S5_GROUP = 16
S5_GROUPS = D_MODEL // S5_GROUP
S5_STATE = 64
S5_NSTATE = S5_GROUPS * S5_STATE
D_FF = 4 * D_MODEL
EPS = 1e-6

SUBLANES = 8
LANES = 128
T_RET = 512
INPROJ_ROW_GROUPS = 2
MLP_STEPS = 64
S5_STEPS = 64
S5_SUBTILES = 2
S5_COLS = 512
CONV_ROWS = 32
CONV_COPY_ROWS = 64
VMEM_LIMIT = 56 * 1024 * 1024


def _cparams(sem):
    return pltpu.CompilerParams(dimension_semantics=sem, vmem_limit_bytes=VMEM_LIMIT)


def _const_spec(shape):
    nd = len(shape)
    return pl.BlockSpec(shape, lambda *_: (0,) * nd, pipeline_mode=pl.Buffered(1))


def _rms(x):
    return x * lax.rsqrt(jnp.mean(x * x, axis=-1, keepdims=True) + EPS)


def _dot(a, b):
    return jnp.dot(a, b, preferred_element_type=F32)


def _ffn(hb, w1_ref, w2_ref):
    d = D_MODEL
    us = []
    for j in range(0, D_FF, d):
        u = jnp.maximum(_dot(hb, w1_ref[:, j:j + d]), 0.0)
        us.append((u * u).astype(BF16))
    return _dot(jnp.concatenate(us, axis=1), w2_ref[...])


def _mod_kernel(c_ref, w_ref, b_ref, o_ref):
    c = c_ref[...]
    ca = c * jax.nn.sigmoid(c)
    w = w_ref[0]
    ch = ca.astype(BF16)
    cl = (ca - ch.astype(F32)).astype(BF16)
    wh = w.astype(BF16)
    wl = (w - wh.astype(F32)).astype(BF16)
    o_ref[0] = _dot(ch, wh) + _dot(cl, wh) + _dot(ch, wl) + b_ref[0]


def _modulation(c, ada_w, ada_b):
    depth, d, d6 = ada_w.shape
    bsz = c.shape[0]
    nblk = d6 // d
    return pl.pallas_call(
        _mod_kernel,
        grid=(depth, nblk),
        in_specs=[
            pl.BlockSpec((bsz, d), lambda l, j: (0, 0)),
            pl.BlockSpec((1, d, d), lambda l, j: (l, 0, j)),
            pl.BlockSpec((1, 1, d), lambda l, j: (l, 0, j)),
        ],
        out_specs=pl.BlockSpec((1, bsz, d), lambda l, j: (l, 0, j)),
        out_shape=jax.ShapeDtypeStruct((depth, bsz, d6), F32),
        compiler_params=_cparams(("arbitrary", "arbitrary")),
        name="adaln_mod",
    )(c, ada_w, ada_b.reshape(depth, 1, d6))


def _retention_tables(tile):
    h = np.arange(RET_HEADS, dtype=np.float64)
    log_g = np.log1p(-np.exp2(-5.0 - h))
    idx = np.arange(tile, dtype=np.float64)
    dist = np.abs(idx[:, None] - idx[None, :])
    dmat = np.exp(log_g[:, None, None] * dist)

    def lanes(w):
        return np.broadcast_to(w[:, :, None], (RET_HEADS, tile, RET_HEAD_DIM))

    w_kf = lanes(np.exp(log_g[:, None] * (tile - 1.0 - idx)[None, :]))
    w_kb = lanes(np.exp(log_g[:, None] * idx[None, :]))
    w_qf = lanes(np.exp(log_g[:, None] * (idx + 1.0)[None, :]))
    w_qb = lanes(np.exp(log_g[:, None] * (tile - idx)[None, :]))
    g_tile = np.exp(log_g * tile)
    f = lambda a: jnp.asarray(np.ascontiguousarray(a), dtype=F32)
    return f(dmat), f(w_kf), f(w_kb), f(w_qf), f(w_qb), f(g_tile)


def _rotary_tables(seq):
    half = RET_HEAD_DIM // 2
    inv = ROPE_BASE ** (-np.arange(0, RET_HEAD_DIM, 2, dtype=np.float64) / RET_HEAD_DIM)
    ang = np.arange(seq, dtype=np.float64)[:, None] * inv[None, :]
    cos, sin = np.cos(ang), np.sin(ang)
    cosf = np.concatenate([cos, cos], axis=1)
    sinf = np.concatenate([-sin, sin], axis=1)
    assert cosf.shape == (seq, 2 * half)
    return jnp.asarray(cosf, dtype=F32), jnp.asarray(sinf, dtype=F32)


def _inproj_kernel(x_ref, mod_ref, ng_ref, win_ref, cos_ref, sin_ref, wkf_ref, wkb_ref,
                   a_ref, q_ref, k_ref, v_ref, g_ref, kvf_ref, kvb_ref):
    d = D_MODEL
    t = x_ref.shape[1]
    mod = mod_ref[0]
    shift, scale = mod[:, 0:d], mod[:, d:2 * d]
    half = RET_HEAD_DIM // 2
    base = 2 * MIX_A
    tn = (((0,), (0,)), ((), ()))
    kvf = [0.0] * RET_HEADS
    kvb = [0.0] * RET_HEADS
    rg = t // INPROJ_ROW_GROUPS
    for r0 in range(0, t, rg):
        rows = slice(r0, r0 + rg)
        h = _rms(x_ref[0, rows, :]) * ng_ref[...] * (1.0 + scale) + shift
        hb = h.astype(BF16)

        a_val = _dot(hb, win_ref[:, 0:MIX_A])
        a_gate = _dot(hb, win_ref[:, MIX_A:2 * MIX_A])
        a_ref[0, rows, :] = a_val * jax.nn.sigmoid(a_gate)

        cosf = cos_ref[rows, :]
        sinf = sin_ref[rows, :]
        g_ref[0, rows, :] = _dot(
            hb, win_ref[:, base + 3 * RET_WIDTH:base + 4 * RET_WIDTH]).astype(BF16)
        v_all = _dot(hb, win_ref[:, base + 2 * RET_WIDTH:base + 3 * RET_WIDTH]).astype(BF16)
        v_ref[0, rows, :] = v_all
        q_all = _dot(hb, win_ref[:, base:base + RET_WIDTH])
        k_all = _dot(hb, win_ref[:, base + RET_WIDTH:base + 2 * RET_WIDTH])
        for hd in range(RET_HEADS):
            lo = hd * RET_HEAD_DIM
            hi = lo + RET_HEAD_DIM
            qh = q_all[:, lo:hi]
            kh = k_all[:, lo:hi]
            qh = (qh * cosf + pltpu.roll(qh, half, 1) * sinf) * (RET_HEAD_DIM ** -0.5)
            kh = kh * cosf + pltpu.roll(kh, half, 1) * sinf
            vb = v_all[:, lo:hi]
            q_ref[0, rows, lo:hi] = qh.astype(BF16)
            k_ref[0, rows, lo:hi] = kh.astype(BF16)
            kvf[hd] = kvf[hd] + lax.dot_general((kh * wkf_ref[hd, rows, :]).astype(BF16), vb, tn,
                                                preferred_element_type=F32)
            kvb[hd] = kvb[hd] + lax.dot_general((kh * wkb_ref[hd, rows, :]).astype(BF16), vb, tn,
                                                preferred_element_type=F32)
    for hd in range(RET_HEADS):
        kvf_ref[0, 0, hd] = kvf[hd]
        kvb_ref[0, 0, hd] = kvb[hd]


def _inproj(x, mod0, ng, w_in, cosf, sinf, w_kf, w_kb):
    bsz, seq, d = x.shape
    t = T_RET
    nt = seq // t
    hd = RET_HEAD_DIM
    row = lambda b, i: (b, i, 0)
    kv_spec = pl.BlockSpec((1, 1, RET_HEADS, hd, hd), lambda b, i: (b, i, 0, 0, 0))
    kv_shape = jax.ShapeDtypeStruct((bsz, nt, RET_HEADS, hd, hd), F32)
    return pl.pallas_call(
        _inproj_kernel,
        grid=(bsz, nt),
        in_specs=[
            pl.BlockSpec((1, t, d), row),
            pl.BlockSpec((1, 1, 6 * d), lambda b, i: (b, 0, 0)),
            _const_spec((1, d)),
            _const_spec((d, IN_COLS)),
            pl.BlockSpec((t, hd), lambda b, i: (i, 0)),
            pl.BlockSpec((t, hd), lambda b, i: (i, 0)),
            _const_spec((RET_HEADS, t, hd)),
            _const_spec((RET_HEADS, t, hd)),
        ],
        out_specs=[
            pl.BlockSpec((1, t, MIX_A), row),
            pl.BlockSpec((1, t, RET_WIDTH), row),
            pl.BlockSpec((1, t, RET_WIDTH), row),
            pl.BlockSpec((1, t, RET_WIDTH), row),
            pl.BlockSpec((1, t, RET_WIDTH), row),
            kv_spec, kv_spec,
        ],
        out_shape=[
            jax.ShapeDtypeStruct((bsz, seq, MIX_A), F32),
            jax.ShapeDtypeStruct((bsz, seq, RET_WIDTH), BF16),
            jax.ShapeDtypeStruct((bsz, seq, RET_WIDTH), BF16),
            jax.ShapeDtypeStruct((bsz, seq, RET_WIDTH), BF16),
            jax.ShapeDtypeStruct((bsz, seq, RET_WIDTH), BF16),
            kv_shape, kv_shape,
        ],
        compiler_params=_cparams(("arbitrary", "arbitrary")),
        name="l0_inproj",
    )(x, mod0, ng, w_in, cosf, sinf, w_kf, w_kb)


def _retstate_kernel(gl_ref, kvf_ref, kvb_ref, s_ref):
    g = gl_ref[pl.program_id(1)]
    nt = kvf_ref.shape[1]
    hd = RET_HEAD_DIM
    s = jnp.zeros((hd, hd), F32)
    for c in range(nt):
        s_ref[0, c, 0, :, 0:hd] = s.astype(BF16)
        s = g * s + kvf_ref[0, c, 0]
    s = jnp.zeros((hd, hd), F32)
    for c in reversed(range(nt)):
        s_ref[0, c, 0, :, hd:2 * hd] = s.astype(BF16)
        s = g * s + kvb_ref[0, c, 0]


def _retstate(g_tile, kvf, kvb):
    bsz, nt, nh, hd, _ = kvf.shape
    kv_spec = pl.BlockSpec((1, nt, 1, hd, hd), lambda b, h: (b, 0, h, 0, 0))
    return pl.pallas_call(
        _retstate_kernel,
        grid=(bsz, nh),
        in_specs=[pl.BlockSpec(memory_space=pltpu.SMEM), kv_spec, kv_spec],
        out_specs=pl.BlockSpec((1, nt, 1, hd, 2 * hd), lambda b, h: (b, 0, h, 0, 0)),
        out_shape=jax.ShapeDtypeStruct((bsz, nt, nh, hd, 2 * hd), BF16),
        compiler_params=_cparams(("arbitrary", "arbitrary")),
        name="l0_retstate",
    )(g_tile, kvf, kvb)


def _mixer0_kernel(x_ref, mod_ref, a_ref, ap_ref, an_ref, q_ref, k_ref, v_ref, g_ref,
                   s_ref, dmat_ref, wqf_ref, wqb_ref, cw_ref, cb_ref, lg_ref, lb_ref, wout_ref,
                   ng2_ref, w1_ref, w2_ref, o_ref, aext, ash, cat):
    d = D_MODEL
    t = a_ref.shape[1]
    i = pl.program_id(1)
    last = pl.num_programs(1) - 1
    halo = CONV_HALO

    aext[0:halo, :] = jnp.where(i > 0, ap_ref[0], 0.0)
    aext[halo:halo + t, :] = a_ref[0]
    aext[halo + t:halo + t + halo, :] = jnp.where(i < last, an_ref[0], 0.0)

    off = halo - CONV_WIDTH // 2
    span = t + SUBLANES * ((off + CONV_WIDTH - 1) // SUBLANES)
    for s in range(1, SUBLANES):
        for r0 in range(0, span, CONV_COPY_ROWS):
            n = min(CONV_COPY_ROWS, span - r0)
            ash[s - 1, r0:r0 + n, :] = aext[r0 + s:r0 + s + n, :]

    cb = cb_ref[...]
    lg = lg_ref[...]
    lb = lb_ref[...]

    def conv_rows(lo, hi):
        for r0 in range(lo, hi, CONV_ROWS):
            acc = jnp.zeros((CONV_ROWS, MIX_A), F32)
            for kk in range(CONV_WIDTH):
                s = (off + kk) % SUBLANES
                base = r0 + (off + kk) - s
                if s == 0:
                    tap = aext[base:base + CONV_ROWS, :]
                else:
                    tap = ash[s - 1, base:base + CONV_ROWS, :]
                acc = acc + tap * cw_ref[kk:kk + 1, :]
            acc = acc + cb
            mu = jnp.mean(acc, axis=-1, keepdims=True)
            xc = acc - mu
            var = jnp.mean(xc * xc, axis=-1, keepdims=True)
            y = xc * lax.rsqrt(var + EPS) * lg + lb
            cat[r0:r0 + CONV_ROWS, 0:MIX_A] = (y * jax.nn.sigmoid(y)).astype(BF16)

    hd = RET_HEAD_DIM
    nt_dims = (((1,), (1,)), ((), ()))
    for h in range(RET_HEADS):
        lo = h * hd
        qh = q_ref[0, :, lo:lo + hd]
        kh = k_ref[0, :, lo:lo + hd]
        vh = v_ref[0, :, lo:lo + hd]
        sc = lax.dot_general(qh, kh, nt_dims, preferred_element_type=F32)
        p = (sc * dmat_ref[h]).astype(BF16)
        r = _dot(p, vh)
        cross = _dot(qh, s_ref[0, 0, h])
        r = r + wqf_ref[h] * cross[:, 0:hd] + wqb_ref[h] * cross[:, hd:2 * hd]
        r = _rms(r)
        gh = g_ref[0, :, lo:lo + hd].astype(F32)
        cat[:, MIX_A + lo:MIX_A + lo + hd] = (gh * jax.nn.sigmoid(gh) * r).astype(BF16)

    conv_rows(0, t)
    mod = mod_ref[0]
    x1 = x_ref[0] + mod[:, 2 * d:3 * d] * _dot(cat[...], wout_ref[...])

    shift, scale, gate = mod[:, 3 * d:4 * d], mod[:, 4 * d:5 * d], mod[:, 5 * d:6 * d]
    hb = (_rms(x1) * ng2_ref[...] * (1.0 + scale) + shift).astype(BF16)
    o_ref[0] = x1 + gate * _ffn(hb, w1_ref, w2_ref)


def _mixer0(x, mod0, a, q, k, v, g, states, dmat, w_qf, w_qb, conv_w, conv_b, cln_g, cln_b,
            w_out, ng2, w1, w2):
    bsz, seq, d = x.shape
    t = T_RET
    nt = seq // t
    hd = RET_HEAD_DIM
    hb = t // CONV_HALO
    nhalo = seq // CONV_HALO
    row = lambda b, i: (b, i, 0)
    return pl.pallas_call(
        _mixer0_kernel,
        grid=(bsz, nt),
        in_specs=[
            pl.BlockSpec((1, t, d), row),
            pl.BlockSpec((1, 1, 6 * d), lambda b, i: (b, 0, 0)),
            pl.BlockSpec((1, t, MIX_A), row),
            pl.BlockSpec((1, CONV_HALO, MIX_A), lambda b, i: (b, jnp.maximum(i * hb - 1, 0), 0)),
            pl.BlockSpec((1, CONV_HALO, MIX_A),
                         lambda b, i: (b, jnp.minimum((i + 1) * hb, nhalo - 1), 0)),
            pl.BlockSpec((1, t, RET_WIDTH), row),
            pl.BlockSpec((1, t, RET_WIDTH), row),
            pl.BlockSpec((1, t, RET_WIDTH), row),
            pl.BlockSpec((1, t, RET_WIDTH), row),
            pl.BlockSpec((1, 1, RET_HEADS, hd, 2 * hd), lambda b, i: (b, i, 0, 0, 0)),
            _const_spec((RET_HEADS, t, t)),
            _const_spec((RET_HEADS, t, hd)),
            _const_spec((RET_HEADS, t, hd)),
            _const_spec((CONV_WIDTH + 1, MIX_A)),
            _const_spec((1, MIX_A)),
            _const_spec((1, MIX_A)),
            _const_spec((1, MIX_A)),
            _const_spec((d, d)),
            _const_spec((1, d)),
            _const_spec((d, D_FF)),
            _const_spec((D_FF, d)),
        ],
        out_specs=pl.BlockSpec((1, t, d), row),
        out_shape=jax.ShapeDtypeStruct((bsz, seq, d), F32),
        scratch_shapes=[
            pltpu.VMEM((t + 2 * CONV_HALO, MIX_A), F32),
            pltpu.VMEM((SUBLANES - 1, t + 2 * CONV_HALO, MIX_A), F32),
            pltpu.VMEM((t, d), BF16),
        ],
        compiler_params=_cparams(("arbitrary", "arbitrary")),
        name="l0_mixer_mlp",
    )(x, mod0, a, a, a, q, k, v, g, states, dmat, w_qf, w_qb, conv_w, conv_b, cln_g, cln_b,
      w_out, ng2, w1, w2)


def _mlp1_kernel(x_ref, mod_ref, ng_ref, w1_ref, w2_ref, nf_ref, o_ref, xs):
    d = D_MODEL
    steps = MLP_STEPS
    nb = SUBLANES
    for b in range(nb):
        xs[b * steps:(b + 1) * steps, :] = x_ref[:, b, :]
    x = xs[...]
    xn = _rms(x) * ng_ref[...]
    hs = []
    for b in range(nb):
        mod = mod_ref[b]
        shift, scale = mod[:, 3 * d:4 * d], mod[:, 4 * d:5 * d]
        hs.append((xn[b * steps:(b + 1) * steps] * (1.0 + scale) + shift).astype(BF16))
    acc = _ffn(jnp.concatenate(hs, axis=0), w1_ref, w2_ref)
    for b in range(nb):
        gate = mod_ref[b][:, 5 * d:6 * d]
        y = x[b * steps:(b + 1) * steps] + gate * acc[b * steps:(b + 1) * steps]
        o_ref[b] = _rms(y) * nf_ref[...]


def _mlp1(x_tm, mod, ng, w1, w2, nf):
    seq, bsz, d = x_tm.shape
    steps = MLP_STEPS
    return pl.pallas_call(
        _mlp1_kernel,
        grid=(seq // steps,),
        in_specs=[
            pl.BlockSpec((steps, bsz, d), lambda i: (i, 0, 0)),
            _const_spec((bsz, 1, 6 * d)),
            _const_spec((1, d)),
            _const_spec((d, D_FF)),
            _const_spec((D_FF, d)),
            _const_spec((1, d)),
        ],
        out_specs=pl.BlockSpec((bsz, steps, d), lambda i: (0, i, 0)),
        out_shape=jax.ShapeDtypeStruct((bsz, seq, d), F32),
        scratch_shapes=[pltpu.VMEM((steps * bsz, d), F32)],
        compiler_params=_cparams(("arbitrary",)),
        name="l1_mlp_final",
    )(x_tm, mod, ng, w1, w2, nf)


def _s5_weights(lam_re, lam_im, log_step, b_re, b_im, c_re, c_im):
    lr = jnp.minimum(lam_re.astype(F32), -1e-4)
    li = lam_im.astype(F32)
    dt = jnp.exp(log_step.astype(F32))[..., None]
    mag = jnp.exp(lr * dt)
    ab_re, ab_im = mag * jnp.cos(li * dt), mag * jnp.sin(li * dt)
    den = lr * lr + li * li
    nr, ni = ab_re - 1.0, ab_im
    f_re = (nr * lr + ni * li) / den
    f_im = (ni * lr - nr * li) / den
    br_, bi_ = b_re.astype(F32), b_im.astype(F32)
    bb_re = f_re[..., None] * br_ - f_im[..., None] * bi_
    bb_im = f_re[..., None] * bi_ + f_im[..., None] * br_

    cr, ci = c_re.astype(F32), c_im.astype(F32)

    def cmul(xr, xi, yr, yi):
        return xr * yr - xi * yi, xr * yi + xi * yr

    a2_re, a2_im = cmul(ab_re, ab_im, ab_re, ab_im)
    abb_re, abb_im = cmul(ab_re[..., None], ab_im[..., None], bb_re, bb_im)
    ca_re, ca_im = cmul(cr, ci, ab_re[:, :, None, :], ab_im[:, :, None, :])
    ca2_re, ca2_im = cmul(cr, ci, a2_re[:, :, None, :], a2_im[:, :, None, :])

    def re_cb(br, bi):
        hp = lax.Precision.HIGHEST
        return (jnp.einsum('dgcp,dgpk->dgck', cr, br, precision=hp)
                - jnp.einsum('dgcp,dgpk->dgck', ci, bi, precision=hp))

    k0 = re_cb(bb_re, bb_im)
    k1 = re_cb(abb_re, abb_im)

    gpo = LANES // S5_GROUP
    noct = S5_GROUPS // gpo
    eye = jnp.eye(gpo, dtype=F32)

    def pack(blocks, spec, rows, cols):
        z = jnp.stack([jnp.stack(r) for r in blocks])
        z = z.reshape(z.shape[:3] + (noct, gpo) + z.shape[4:])
        return jnp.einsum(spec, z, eye).reshape(2, noct, rows, cols).astype(BF16)

    ost = gpo * S5_STATE
    wb = pack([[abb_re, abb_im], [bb_re, bb_im]], 'abdogpc,gh->doagcbhp', 2 * LANES, 2 * ost)
    wch = pack([[ca_re, ca2_re], [-ca_im, -ca2_im]], 'abdogcp,gh->doagpbhc', 2 * ost, 2 * LANES)
    wcu = pack([[k0, k1], [jnp.zeros_like(k0), k0]], 'abdogck,gh->doagkbhc', 2 * LANES, 2 * LANES)
    a2r = jnp.broadcast_to(a2_re.reshape(2, 1, S5_NSTATE), (2, SUBLANES, S5_NSTATE))
    a2i = jnp.broadcast_to(a2_im.reshape(2, 1, S5_NSTATE), (2, SUBLANES, S5_NSTATE))
    return a2r, a2i, wb, wch, wcu


def _s5_direction(x_pair, mod_ref, ng_ref, a2r_ref, a2i_ref, wb_ref, wch_ref, wcu_ref,
                  hr_buf, hi_buf, *, reverse):
    d = D_MODEL
    rows = x_pair[0].shape[0]
    pairs = rows // SUBLANES
    first, second = (1, 0) if reverse else (0, 1)

    mod = mod_ref[...]
    shift, scale = mod[:, 0:d], mod[:, d:2 * d]

    def modulate(x):
        xn = (_rms(x) * ng_ref[...]).reshape(pairs, SUBLANES, d)
        return (xn * (1.0 + scale)[None] + shift[None]).reshape(rows, d)

    u_pair = [modulate(x_pair[0]), modulate(x_pair[1])]
    ub = [u.astype(BF16) for u in u_pair]

    noct = wb_ref.shape[1]
    ost = wb_ref.shape[3] // 2
    if reverse:
        own, prev, carry_src, carry_dst = slice(0, rows), slice(SUBLANES, rows + SUBLANES), \
            slice(0, SUBLANES), slice(rows, rows + SUBLANES)
    else:
        own, prev, carry_src, carry_dst = slice(SUBLANES, rows + SUBLANES), slice(0, rows), \
            slice(rows, rows + SUBLANES), slice(0, SUBLANES)
    hr_buf[carry_dst, :] = hr_buf[carry_src, :]
    hi_buf[carry_dst, :] = hi_buf[carry_src, :]

    def lhs_u(o):
        lanes = slice(o * LANES, (o + 1) * LANES)
        return jnp.concatenate([ub[first][:, lanes], ub[second][:, lanes]], axis=1)

    def project_in(o):
        r = _dot(lhs_u(o), wb_ref[0, o])
        hr_buf[own, o * ost:(o + 1) * ost] = r[:, 0:ost]
        hi_buf[own, o * ost:(o + 1) * ost] = r[:, ost:2 * ost]

    def scan(o):
        for j in range(o * ost, (o + 1) * ost, S5_COLS):
            cols = slice(j, j + S5_COLS)
            ar = a2r_ref[0, :, cols]
            ai = a2i_ref[0, :, cols]
            hr = hr_buf[carry_dst, cols]
            hi = hi_buf[carry_dst, cols]
            for s in range(pairs):
                p = (pairs - 1 - s) if reverse else s
                rws = slice(own.start + p * SUBLANES, own.start + (p + 1) * SUBLANES)
                nr = ar * hr - ai * hi + hr_buf[rws, cols]
                ni = ar * hi + ai * hr + hi_buf[rws, cols]
                hr_buf[rws, cols] = nr
                hi_buf[rws, cols] = ni
                hr, hi = nr, ni

    def project_out(o):
        st = slice(o * ost, (o + 1) * ost)
        return (_dot(hr_buf[prev, st].astype(BF16), wch_ref[0, o, 0:ost, :])
                + _dot(hi_buf[prev, st].astype(BF16), wch_ref[0, o, ost:2 * ost, :])
                + _dot(lhs_u(o), wcu_ref[0, o]))

    outs = [None] * noct
    project_in(0)
    for o in range(noct):
        if o + 1 < noct:
            project_in(o + 1)
        scan(o)
        outs[o] = project_out(o)
    y_pair = [None, None]
    y_pair[first] = jnp.concatenate([r[:, 0:LANES] for r in outs], axis=1)
    y_pair[second] = jnp.concatenate([r[:, LANES:2 * LANES] for r in outs], axis=1)
    return u_pair, y_pair


def _s5_init_state(hr_buf, hi_buf):
    @pl.when(pl.program_id(0) == 0)
    def _():
        hr_buf[...] = jnp.zeros_like(hr_buf)
        hi_buf[...] = jnp.zeros_like(hi_buf)


def _s5_fwd_kernel(x_ref, mod_ref, ng_ref, a2r_ref, a2i_ref, wb_ref, wch_ref, wcu_ref,
                   yf_ref, xtm_ref, hr_buf, hi_buf, xs):
    d = D_MODEL
    pairs = S5_STEPS // 2
    rows = pairs * SUBLANES
    _s5_init_state(hr_buf, hi_buf)
    for b in range(SUBLANES):
        xs[:, b, :] = x_ref[b]
    for sub in range(S5_SUBTILES):
        x4 = xs[sub * S5_STEPS:(sub + 1) * S5_STEPS].reshape(pairs, 2, SUBLANES, d)
        xtm_ref[sub * pairs:(sub + 1) * pairs] = x4
        x_pair = [x4[:, ph].reshape(rows, d) for ph in range(2)]
        _, y_pair = _s5_direction(x_pair, mod_ref, ng_ref, a2r_ref, a2i_ref, wb_ref, wch_ref,
                                  wcu_ref, hr_buf, hi_buf, reverse=False)
        for ph in range(2):
            yf_ref[sub * pairs:(sub + 1) * pairs, ph] = y_pair[ph].reshape(pairs, SUBLANES, d)


def _s5_bwd_kernel(x_ref, mod_ref, ng_ref, a2r_ref, a2i_ref, wb_ref, wch_ref, wcu_ref, yf_ref,
                   dsk_ref, wa_ref, wg_ref, o_ref, hr_buf, hi_buf):
    d = D_MODEL
    pairs = S5_STEPS // 2
    rows = pairs * SUBLANES
    _s5_init_state(hr_buf, hi_buf)
    gate = mod_ref[...][:, 2 * d:3 * d]
    for sub in reversed(range(S5_SUBTILES)):
        prs = slice(sub * pairs, (sub + 1) * pairs)
        x_pair = [x_ref[prs, ph].reshape(rows, d) for ph in range(2)]
        u_pair, yb_pair = _s5_direction(x_pair, mod_ref, ng_ref, a2r_ref, a2i_ref, wb_ref,
                                        wch_ref, wcu_ref, hr_buf, hi_buf, reverse=True)
        zs = []
        for ph in range(2):
            y = yf_ref[prs, ph].reshape(rows, d) + yb_pair[ph] + dsk_ref[...] * u_pair[ph]
            zs.append(jax.nn.gelu(y).astype(BF16))
        zb = jnp.concatenate(zs, axis=0)
        glu = _dot(zb, wa_ref[...]) * jax.nn.sigmoid(_dot(zb, wg_ref[...]))
        for ph in range(2):
            g = glu[ph * rows:(ph + 1) * rows].reshape(pairs, SUBLANES, d)
            o_ref[prs, ph] = x_pair[ph].reshape(pairs, SUBLANES, d) + gate[None] * g


def _s5_layer(x, mod1, ng, a2r, a2i, wb, wch, wcu, d_skip, w_glu_a, w_glu_b):
    bsz, seq, d = x.shape
    sub_rows = S5_STEPS // 2 * SUBLANES
    tile_steps = S5_STEPS * S5_SUBTILES
    nt = seq // tile_steps
    scratch = [
        pltpu.VMEM((sub_rows + SUBLANES, S5_NSTATE), F32),
        pltpu.VMEM((sub_rows + SUBLANES, S5_NSTATE), F32),
    ]

    def dir_spec(w):
        nd = w.ndim - 1
        return lambda direction: pl.BlockSpec((1,) + w.shape[1:],
                                              lambda i: (direction,) + (0,) * nd,
                                              pipeline_mode=pl.Buffered(1))

    def dir_specs(direction, x_spec):
        return [x_spec, _const_spec((SUBLANES, 6 * d)), _const_spec((1, d))] + [
            dir_spec(w)(direction) for w in (a2r, a2i, wb, wch, wcu)]

    bwd = lambda i: nt - 1 - i
    tm_block = (tile_steps // 2, 2, bsz, d)
    tm_shape = jax.ShapeDtypeStruct((seq // 2, 2, bsz, d), F32)
    yf, x_tm = pl.pallas_call(
        _s5_fwd_kernel,
        grid=(nt,),
        in_specs=dir_specs(0, pl.BlockSpec((bsz, tile_steps, d), lambda i: (0, i, 0))),
        out_specs=[pl.BlockSpec(tm_block, lambda i: (i, 0, 0, 0)),
                   pl.BlockSpec(tm_block, lambda i: (i, 0, 0, 0))],
        out_shape=[tm_shape, tm_shape],
        scratch_shapes=scratch + [pltpu.VMEM((tile_steps, SUBLANES, d), F32)],
        compiler_params=_cparams(("arbitrary",)),
        name="s5_fwd",
    )(x, mod1, ng, a2r, a2i, wb, wch, wcu)
    return pl.pallas_call(
        _s5_bwd_kernel,
        grid=(nt,),
        in_specs=dir_specs(1, pl.BlockSpec(tm_block, lambda i: (bwd(i), 0, 0, 0))) + [
            pl.BlockSpec(tm_block, lambda i: (bwd(i), 0, 0, 0)),
            _const_spec((1, d)),
            _const_spec((d, d)),
            _const_spec((d, d)),
        ],
        out_specs=pl.BlockSpec(tm_block, lambda i: (bwd(i), 0, 0, 0)),
        out_shape=tm_shape,
        scratch_shapes=scratch,
        compiler_params=_cparams(("arbitrary",)),
        name="s5_bwd",
    )(x_tm, mod1, ng, a2r, a2i, wb, wch, wcu, yf, d_skip, w_glu_a, w_glu_b)


def kernel(x, c, norm_g, ada_w, ada_b, w_in, conv_w, conv_b, cln_g, cln_b, w_out, s5_lam_re,
           s5_lam_im, s5_log_step, s5_b_re, s5_b_im, s5_c_re, s5_c_im, s5_d, w_glu_a, w_glu_b,
           w_fc1, w_fc2, norm_f):
    bsz, seq, d = x.shape
    assert d == D_MODEL and bsz == SUBLANES
    assert seq % T_RET == 0 and seq % MLP_STEPS == 0
    assert seq % (S5_STEPS * S5_SUBTILES) == 0
    assert norm_g.shape[0] == 2, "one conv/retention layer followed by one S5 layer"

    mod = _modulation(c, ada_w, ada_b)
    mod0 = mod[0].reshape(bsz, 1, 6 * d)
    mod1 = mod[1]
    row = lambda v: v.reshape(1, -1).astype(F32)

    dmat, w_kf, w_kb, w_qf, w_qb, g_tile = _retention_tables(T_RET)
    cosf, sinf = _rotary_tables(seq)
    a, q, k, v, g, kvf, kvb = _inproj(x, mod0, row(norm_g[0, 0]), w_in[0].astype(BF16),
                                      cosf, sinf, w_kf, w_kb)
    states = _retstate(g_tile, kvf, kvb)
    cw = jnp.concatenate([conv_w[0], jnp.zeros((1, MIX_A), conv_w.dtype)], axis=0)
    x = _mixer0(x, mod0, a, q, k, v, g, states, dmat, w_qf, w_qb, cw, row(conv_b[0]),
                row(cln_g[0]), row(cln_b[0]), w_out[0].astype(BF16), row(norm_g[0, 1]),
                w_fc1[0].astype(BF16), w_fc2[0].astype(BF16))

    s5w = _s5_weights(s5_lam_re[0], s5_lam_im[0], s5_log_step[0], s5_b_re[0],
                      s5_b_im[0], s5_c_re[0], s5_c_im[0])
    x = _s5_layer(x, mod1, row(norm_g[1, 0]), *s5w,
                  row(s5_d[0]), w_glu_a[0].astype(BF16), w_glu_b[0].astype(BF16))
    return _mlp1(x.reshape(seq, bsz, d), mod1.reshape(bsz, 1, 6 * d), row(norm_g[1, 1]),
                 w_fc1[1].astype(BF16), w_fc2[1].astype(BF16), row(norm_f))
```

```python
import numpy as np
import jax
import jax.numpy as jnp
from jax import lax
from jax.experimental import pallas as pl
from jax.experimental.pallas import tpu as pltpu

F32 = jnp.float32
BF16 = jnp.bfloat16

D_MODEL = 1024
MIX_A = D_MODEL // 2
RET_HEADS = 4
RET_HEAD_DIM = 128
RET_WIDTH = RET_HEADS * RET_HEAD_DIM
IN_COLS = 2 * MIX_A + 4 * RET_WIDTH
CONV_WIDTH = 31
CONV_HALO = 16
ROPE_BASE = 10000.0
S5_GROUP = 16
S5_GROUPS = D_MODEL // S5_GROUP
S5_STATE = 64
S5_NSTATE = S5_GROUPS * S5_STATE
D_FF = 4 * D_MODEL
EPS = 1e-6

SUBLANES = 8
LANES = 128
T_RET = 512
INPROJ_ROW_GROUPS = 2
MLP_STEPS = 64
S5_STEPS = 64
S5_SUBTILES = 2
S5_COLS = 512
CONV_ROWS = 32
CONV_COPY_ROWS = 64
VMEM_LIMIT = 56 * 1024 * 1024


def _cparams(sem):
    return pltpu.CompilerParams(dimension_semantics=sem, vmem_limit_bytes=VMEM_LIMIT)


def _const_spec(shape):
    nd = len(shape)
    return pl.BlockSpec(shape, lambda *_: (0,) * nd, pipeline_mode=pl.Buffered(1))


def _rms(x):
    return x * lax.rsqrt(jnp.mean(x * x, axis=-1, keepdims=True) + EPS)


def _dot(a, b):
    return jnp.dot(a, b, preferred_element_type=F32)


def _ffn(hb, w1_ref, w2_ref):
    d = D_MODEL
    us = []
    for j in range(0, D_FF, d):
        u = jnp.maximum(_dot(hb, w1_ref[:, j:j + d]), 0.0)
        us.append((u * u).astype(BF16))
    return _dot(jnp.concatenate(us, axis=1), w2_ref[...])


def _mod_kernel(c_ref, w_ref, b_ref, o_ref):
    c = c_ref[...]
    ca = c * jax.nn.sigmoid(c)
    w = w_ref[0]
    ch = ca.astype(BF16)
    cl = (ca - ch.astype(F32)).astype(BF16)
    wh = w.astype(BF16)
    wl = (w - wh.astype(F32)).astype(BF16)
    o_ref[0] = _dot(ch, wh) + _dot(cl, wh) + _dot(ch, wl) + b_ref[0]


def _modulation(c, ada_w, ada_b):
    depth, d, d6 = ada_w.shape
    bsz = c.shape[0]
    nblk = d6 // d
    return pl.pallas_call(
        _mod_kernel,
        grid=(depth, nblk),
        in_specs=[
            pl.BlockSpec((bsz, d), lambda l, j: (0, 0)),
            pl.BlockSpec((1, d, d), lambda l, j: (l, 0, j)),
            pl.BlockSpec((1, 1, d), lambda l, j: (l, 0, j)),
        ],
        out_specs=pl.BlockSpec((1, bsz, d), lambda l, j: (l, 0, j)),
        out_shape=jax.ShapeDtypeStruct((depth, bsz, d6), F32),
        compiler_params=_cparams(("arbitrary", "arbitrary")),
        name="adaln_mod",
    )(c, ada_w, ada_b.reshape(depth, 1, d6))


def _retention_tables(tile):
    h = np.arange(RET_HEADS, dtype=np.float64)
    log_g = np.log1p(-np.exp2(-5.0 - h))
    idx = np.arange(tile, dtype=np.float64)
    dist = np.abs(idx[:, None] - idx[None, :])
    dmat = np.exp(log_g[:, None, None] * dist)

    def lanes(w):
        return np.broadcast_to(w[:, :, None], (RET_HEADS, tile, RET_HEAD_DIM))

    w_kf = lanes(np.exp(log_g[:, None] * (tile - 1.0 - idx)[None, :]))
    w_kb = lanes(np.exp(log_g[:, None] * idx[None, :]))
    w_qf = lanes(np.exp(log_g[:, None] * (idx + 1.0)[None, :]))
    w_qb = lanes(np.exp(log_g[:, None] * (tile - idx)[None, :]))
    g_tile = np.exp(log_g * tile)
    f = lambda a: jnp.asarray(np.ascontiguousarray(a), dtype=F32)
    return f(dmat), f(w_kf), f(w_kb), f(w_qf), f(w_qb), f(g_tile)


def _rotary_tables(seq):
    half = RET_HEAD_DIM // 2
    inv = ROPE_BASE ** (-np.arange(0, RET_HEAD_DIM, 2, dtype=np.float64) / RET_HEAD_DIM)
    ang = np.arange(seq, dtype=np.float64)[:, None] * inv[None, :]
    cos, sin = np.cos(ang), np.sin(ang)
    cosf = np.concatenate([cos, cos], axis=1)
    sinf = np.concatenate([-sin, sin], axis=1)
    assert cosf.shape == (seq, 2 * half)
    return jnp.asarray(cosf, dtype=F32), jnp.asarray(sinf, dtype=F32)


def _inproj_kernel(x_ref, mod_ref, ng_ref, win_ref, cos_ref, sin_ref, wkf_ref, wkb_ref,
                   a_ref, q_ref, k_ref, v_ref, g_ref, kvf_ref, kvb_ref):
    d = D_MODEL
    t = x_ref.shape[1]
    mod = mod_ref[0]
    shift, scale = mod[:, 0:d], mod[:, d:2 * d]
    half = RET_HEAD_DIM // 2
    base = 2 * MIX_A
    tn = (((0,), (0,)), ((), ()))
    kvf = [0.0] * RET_HEADS
    kvb = [0.0] * RET_HEADS
    rg = t // INPROJ_ROW_GROUPS
    for r0 in range(0, t, rg):
        rows = slice(r0, r0 + rg)
        h = _rms(x_ref[0, rows, :]) * ng_ref[...] * (1.0 + scale) + shift
        hb = h.astype(BF16)

        a_val = _dot(hb, win_ref[:, 0:MIX_A])
        a_gate = _dot(hb, win_ref[:, MIX_A:2 * MIX_A])
        a_ref[0, rows, :] = a_val * jax.nn.sigmoid(a_gate)

        cosf = cos_ref[rows, :]
        sinf = sin_ref[rows, :]
        g_ref[0, rows, :] = _dot(
            hb, win_ref[:, base + 3 * RET_WIDTH:base + 4 * RET_WIDTH]).astype(BF16)
        v_all = _dot(hb, win_ref[:, base + 2 * RET_WIDTH:base + 3 * RET_WIDTH]).astype(BF16)
        v_ref[0, rows, :] = v_all
        q_all = _dot(hb, win_ref[:, base:base + RET_WIDTH])
        k_all = _dot(hb, win_ref[:, base + RET_WIDTH:base + 2 * RET_WIDTH])
        for hd in range(RET_HEADS):
            lo = hd * RET_HEAD_DIM
            hi = lo + RET_HEAD_DIM
            qh = q_all[:, lo:hi]
            kh = k_all[:, lo:hi]
            qh = (qh * cosf + pltpu.roll(qh, half, 1) * sinf) * (RET_HEAD_DIM ** -0.5)
            kh = kh * cosf + pltpu.roll(kh, half, 1) * sinf
            vb = v_all[:, lo:hi]
            q_ref[0, rows, lo:hi] = qh.astype(BF16)
            k_ref[0, rows, lo:hi] = kh.astype(BF16)
            kvf[hd] = kvf[hd] + lax.dot_general((kh * wkf_ref[hd, rows, :]).astype(BF16), vb, tn,
                                                preferred_element_type=F32)
            kvb[hd] = kvb[hd] + lax.dot_general((kh * wkb_ref[hd, rows, :]).astype(BF16), vb, tn,
                                                preferred_element_type=F32)
    for hd in range(RET_HEADS):
        kvf_ref[0, 0, hd] = kvf[hd]
        kvb_ref[0, 0, hd] = kvb[hd]


def _inproj(x, mod0, ng, w_in, cosf, sinf, w_kf, w_kb):
    bsz, seq, d = x.shape
    t = T_RET
    nt = seq // t
    hd = RET_HEAD_DIM
    row = lambda b, i: (b, i, 0)
    kv_spec = pl.BlockSpec((1, 1, RET_HEADS, hd, hd), lambda b, i: (b, i, 0, 0, 0))
    kv_shape = jax.ShapeDtypeStruct((bsz, nt, RET_HEADS, hd, hd), F32)
    return pl.pallas_call(
        _inproj_kernel,
        grid=(bsz, nt),
        in_specs=[
            pl.BlockSpec((1, t, d), row),
            pl.BlockSpec((1, 1, 6 * d), lambda b, i: (b, 0, 0)),
            _const_spec((1, d)),
            _const_spec((d, IN_COLS)),
            pl.BlockSpec((t, hd), lambda b, i: (i, 0)),
            pl.BlockSpec((t, hd), lambda b, i: (i, 0)),
            _const_spec((RET_HEADS, t, hd)),
            _const_spec((RET_HEADS, t, hd)),
        ],
        out_specs=[
            pl.BlockSpec((1, t, MIX_A), row),
            pl.BlockSpec((1, t, RET_WIDTH), row),
            pl.BlockSpec((1, t, RET_WIDTH), row),
            pl.BlockSpec((1, t, RET_WIDTH), row),
            pl.BlockSpec((1, t, RET_WIDTH), row),
            kv_spec, kv_spec,
        ],
        out_shape=[
            jax.ShapeDtypeStruct((bsz, seq, MIX_A), F32),
            jax.ShapeDtypeStruct((bsz, seq, RET_WIDTH), BF16),
            jax.ShapeDtypeStruct((bsz, seq, RET_WIDTH), BF16),
            jax.ShapeDtypeStruct((bsz, seq, RET_WIDTH), BF16),
            jax.ShapeDtypeStruct((bsz, seq, RET_WIDTH), BF16),
            kv_shape, kv_shape,
        ],
        compiler_params=_cparams(("arbitrary", "arbitrary")),
        name="l0_inproj",
    )(x, mod0, ng, w_in, cosf, sinf, w_kf, w_kb)


def _retstate_kernel(gl_ref, kvf_ref, kvb_ref, s_ref):
    g = gl_ref[pl.program_id(1)]
    nt = kvf_ref.shape[1]
    hd = RET_HEAD_DIM
    s = jnp.zeros((hd, hd), F32)
    for c in range(nt):
        s_ref[0, c, 0, :, 0:hd] = s.astype(BF16)
        s = g * s + kvf_ref[0, c, 0]
    s = jnp.zeros((hd, hd), F32)
    for c in reversed(range(nt)):
        s_ref[0, c, 0, :, hd:2 * hd] = s.astype(BF16)
        s = g * s + kvb_ref[0, c, 0]


def _retstate(g_tile, kvf, kvb):
    bsz, nt, nh, hd, _ = kvf.shape
    kv_spec = pl.BlockSpec((1, nt, 1, hd, hd), lambda b, h: (b, 0, h, 0, 0))
    return pl.pallas_call(
        _retstate_kernel,
        grid=(bsz, nh),
        in_specs=[pl.BlockSpec(memory_space=pltpu.SMEM), kv_spec, kv_spec],
        out_specs=pl.BlockSpec((1, nt, 1, hd, 2 * hd), lambda b, h: (b, 0, h, 0, 0)),
        out_shape=jax.ShapeDtypeStruct((bsz, nt, nh, hd, 2 * hd), BF16),
        compiler_params=_cparams(("arbitrary", "arbitrary")),
        name="l0_retstate",
    )(g_tile, kvf, kvb)


def _mixer0_kernel(x_ref, mod_ref, a_ref, ap_ref, an_ref, q_ref, k_ref, v_ref, g_ref,
                   s_ref, dmat_ref, wqf_ref, wqb_ref, cw_ref, cb_ref, lg_ref, lb_ref, wout_ref,
                   ng2_ref, w1_ref, w2_ref, o_ref, aext, ash, cat):
    d = D_MODEL
    t = a_ref.shape[1]
    i = pl.program_id(1)
    last = pl.num_programs(1) - 1
    halo = CONV_HALO

    aext[0:halo, :] = jnp.where(i > 0, ap_ref[0], 0.0)
    aext[halo:halo + t, :] = a_ref[0]
    aext[halo + t:halo + t + halo, :] = jnp.where(i < last, an_ref[0], 0.0)

    off = halo - CONV_WIDTH // 2
    span = t + SUBLANES * ((off + CONV_WIDTH - 1) // SUBLANES)
    for s in range(1, SUBLANES):
        for r0 in range(0, span, CONV_COPY_ROWS):
            n = min(CONV_COPY_ROWS, span - r0)
            ash[s - 1, r0:r0 + n, :] = aext[r0 + s:r0 + s + n, :]

    cb = cb_ref[...]
    lg = lg_ref[...]
    lb = lb_ref[...]

    def conv_rows(lo, hi):
        for r0 in range(lo, hi, CONV_ROWS):
            acc = jnp.zeros((CONV_ROWS, MIX_A), F32)
            for kk in range(CONV_WIDTH):
                s = (off + kk) % SUBLANES
                base = r0 + (off + kk) - s
                if s == 0:
                    tap = aext[base:base + CONV_ROWS, :]
                else:
                    tap = ash[s - 1, base:base + CONV_ROWS, :]
                acc = acc + tap * cw_ref[kk:kk + 1, :]
            acc = acc + cb
            mu = jnp.mean(acc, axis=-1, keepdims=True)
            xc = acc - mu
            var = jnp.mean(xc * xc, axis=-1, keepdims=True)
            y = xc * lax.rsqrt(var + EPS) * lg + lb
            cat[r0:r0 + CONV_ROWS, 0:MIX_A] = (y * jax.nn.sigmoid(y)).astype(BF16)

    hd = RET_HEAD_DIM
    nt_dims = (((1,), (1,)), ((), ()))
    for h in range(RET_HEADS):
        lo = h * hd
        qh = q_ref[0, :, lo:lo + hd]
        kh = k_ref[0, :, lo:lo + hd]
        vh = v_ref[0, :, lo:lo + hd]
        sc = lax.dot_general(qh, kh, nt_dims, preferred_element_type=F32)
        p = (sc * dmat_ref[h]).astype(BF16)
        r = _dot(p, vh)
        cross = _dot(qh, s_ref[0, 0, h])
        r = r + wqf_ref[h] * cross[:, 0:hd] + wqb_ref[h] * cross[:, hd:2 * hd]
        r = _rms(r)
        gh = g_ref[0, :, lo:lo + hd].astype(F32)
        cat[:, MIX_A + lo:MIX_A + lo + hd] = (gh * jax.nn.sigmoid(gh) * r).astype(BF16)

    conv_rows(0, t)
    mod = mod_ref[0]
    x1 = x_ref[0] + mod[:, 2 * d:3 * d] * _dot(cat[...], wout_ref[...])

    shift, scale, gate = mod[:, 3 * d:4 * d], mod[:, 4 * d:5 * d], mod[:, 5 * d:6 * d]
    hb = (_rms(x1) * ng2_ref[...] * (1.0 + scale) + shift).astype(BF16)
    o_ref[0] = x1 + gate * _ffn(hb, w1_ref, w2_ref)


def _mixer0(x, mod0, a, q, k, v, g, states, dmat, w_qf, w_qb, conv_w, conv_b, cln_g, cln_b,
            w_out, ng2, w1, w2):
    bsz, seq, d = x.shape
    t = T_RET
    nt = seq // t
    hd = RET_HEAD_DIM
    hb = t // CONV_HALO
    nhalo = seq // CONV_HALO
    row = lambda b, i: (b, i, 0)
    return pl.pallas_call(
        _mixer0_kernel,
        grid=(bsz, nt),
        in_specs=[
            pl.BlockSpec((1, t, d), row),
            pl.BlockSpec((1, 1, 6 * d), lambda b, i: (b, 0, 0)),
            pl.BlockSpec((1, t, MIX_A), row),
            pl.BlockSpec((1, CONV_HALO, MIX_A), lambda b, i: (b, jnp.maximum(i * hb - 1, 0), 0)),
            pl.BlockSpec((1, CONV_HALO, MIX_A),
                         lambda b, i: (b, jnp.minimum((i + 1) * hb, nhalo - 1), 0)),
            pl.BlockSpec((1, t, RET_WIDTH), row),
            pl.BlockSpec((1, t, RET_WIDTH), row),
            pl.BlockSpec((1, t, RET_WIDTH), row),
            pl.BlockSpec((1, t, RET_WIDTH), row),
            pl.BlockSpec((1, 1, RET_HEADS, hd, 2 * hd), lambda b, i: (b, i, 0, 0, 0)),
            _const_spec((RET_HEADS, t, t)),
            _const_spec((RET_HEADS, t, hd)),
            _const_spec((RET_HEADS, t, hd)),
            _const_spec((CONV_WIDTH + 1, MIX_A)),
            _const_spec((1, MIX_A)),
            _const_spec((1, MIX_A)),
            _const_spec((1, MIX_A)),
            _const_spec((d, d)),
            _const_spec((1, d)),
            _const_spec((d, D_FF)),
            _const_spec((D_FF, d)),
        ],
        out_specs=pl.BlockSpec((1, t, d), row),
        out_shape=jax.ShapeDtypeStruct((bsz, seq, d), F32),
        scratch_shapes=[
            pltpu.VMEM((t + 2 * CONV_HALO, MIX_A), F32),
            pltpu.VMEM((SUBLANES - 1, t + 2 * CONV_HALO, MIX_A), F32),
            pltpu.VMEM((t, d), BF16),
        ],
        compiler_params=_cparams(("arbitrary", "arbitrary")),
        name="l0_mixer_mlp",
    )(x, mod0, a, a, a, q, k, v, g, states, dmat, w_qf, w_qb, conv_w, conv_b, cln_g, cln_b,
      w_out, ng2, w1, w2)


def _mlp1_kernel(x_ref, mod_ref, ng_ref, w1_ref, w2_ref, nf_ref, o_ref, xs):
    d = D_MODEL
    steps = MLP_STEPS
    nb = SUBLANES
    for b in range(nb):
        xs[b * steps:(b + 1) * steps, :] = x_ref[:, b, :]
    x = xs[...]
    xn = _rms(x) * ng_ref[...]
    hs = []
    for b in range(nb):
        mod = mod_ref[b]
        shift, scale = mod[:, 3 * d:4 * d], mod[:, 4 * d:5 * d]
        hs.append((xn[b * steps:(b + 1) * steps] * (1.0 + scale) + shift).astype(BF16))
    acc = _ffn(jnp.concatenate(hs, axis=0), w1_ref, w2_ref)
    for b in range(nb):
        gate = mod_ref[b][:, 5 * d:6 * d]
        y = x[b * steps:(b + 1) * steps] + gate * acc[b * steps:(b + 1) * steps]
        o_ref[b] = _rms(y) * nf_ref[...]


def _mlp1(x_tm, mod, ng, w1, w2, nf):
    seq, bsz, d = x_tm.shape
    steps = MLP_STEPS
    return pl.pallas_call(
        _mlp1_kernel,
        grid=(seq // steps,),
        in_specs=[
            pl.BlockSpec((steps, bsz, d), lambda i: (i, 0, 0)),
            _const_spec((bsz, 1, 6 * d)),
            _const_spec((1, d)),
            _const_spec((d, D_FF)),
            _const_spec((D_FF, d)),
            _const_spec((1, d)),
        ],
        out_specs=pl.BlockSpec((bsz, steps, d), lambda i: (0, i, 0)),
        out_shape=jax.ShapeDtypeStruct((bsz, seq, d), F32),
        scratch_shapes=[pltpu.VMEM((steps * bsz, d), F32)],
        compiler_params=_cparams(("arbitrary",)),
        name="l1_mlp_final",
    )(x_tm, mod, ng, w1, w2, nf)


def _s5_weights(lam_re, lam_im, log_step, b_re, b_im, c_re, c_im):
    lr = jnp.minimum(lam_re.astype(F32), -1e-4)
    li = lam_im.astype(F32)
    dt = jnp.exp(log_step.astype(F32))[..., None]
    mag = jnp.exp(lr * dt)
    ab_re, ab_im = mag * jnp.cos(li * dt), mag * jnp.sin(li * dt)
    den = lr * lr + li * li
    nr, ni = ab_re - 1.0, ab_im
    f_re = (nr * lr + ni * li) / den
    f_im = (ni * lr - nr * li) / den
    br_, bi_ = b_re.astype(F32), b_im.astype(F32)
    bb_re = f_re[..., None] * br_ - f_im[..., None] * bi_
    bb_im = f_re[..., None] * bi_ + f_im[..., None] * br_

    cr, ci = c_re.astype(F32), c_im.astype(F32)

    def cmul(xr, xi, yr, yi):
        return xr * yr - xi * yi, xr * yi + xi * yr

    a2_re, a2_im = cmul(ab_re, ab_im, ab_re, ab_im)
    abb_re, abb_im = cmul(ab_re[..., None], ab_im[..., None], bb_re, bb_im)
    ca_re, ca_im = cmul(cr, ci, ab_re[:, :, None, :], ab_im[:, :, None, :])
    ca2_re, ca2_im = cmul(cr, ci, a2_re[:, :, None, :], a2_im[:, :, None, :])

    def re_cb(br, bi):
        hp = lax.Precision.HIGHEST
        return (jnp.einsum('dgcp,dgpk->dgck', cr, br, precision=hp)
                - jnp.einsum('dgcp,dgpk->dgck', ci, bi, precision=hp))

    k0 = re_cb(bb_re, bb_im)
    k1 = re_cb(abb_re, abb_im)

    def blocks(grid):
        t = lambda z: jnp.swapaxes(z, -1, -2)
        return jnp.stack([jnp.stack([t(z) for z in r], axis=2) for r in grid], axis=2)

    b4 = blocks([[abb_re, abb_im], [bb_re, bb_im]])
    c4 = blocks([[ca_re, ca2_re], [-ca_im, -ca2_im]])
    k4 = blocks([[k0, k1], [jnp.zeros_like(k0), k0]])
    wb, wch, wcu = _s5_pack(b4, c4, k4)
    a2r = jnp.broadcast_to(a2_re.reshape(2, 1, S5_NSTATE), (2, SUBLANES, S5_NSTATE))
    a2i = jnp.broadcast_to(a2_im.reshape(2, 1, S5_NSTATE), (2, SUBLANES, S5_NSTATE))
    return a2r, a2i, wb, wch, wcu


def _s5_pack_kernel(b4_ref, c4_ref, k4_ref, wb_ref, wch_ref, wcu_ref):
    gpo = b4_ref.shape[1]
    ost = gpo * S5_STATE
    wb_ref[...] = jnp.zeros_like(wb_ref)
    wch_ref[...] = jnp.zeros_like(wch_ref)
    wcu_ref[...] = jnp.zeros_like(wcu_ref)
    for g in range(gpo):
        ch = slice(g * S5_GROUP, (g + 1) * S5_GROUP)
        st = slice(g * S5_STATE, (g + 1) * S5_STATE)
        for a in range(2):
            for b in range(2):
                off = lambda s, base: slice(base + s.start, base + s.stop)
                wb_ref[0, 0, off(ch, a * LANES), off(st, b * ost)] = b4_ref[0, g, a, b].astype(BF16)
                wch_ref[0, 0, off(st, a * ost), off(ch, b * LANES)] = c4_ref[0, g, a, b].astype(BF16)
                wcu_ref[0, 0, off(ch, a * LANES), off(ch, b * LANES)] = k4_ref[0, g, a, b].astype(BF16)


def _s5_pack(b4, c4, k4):
    gpo = LANES // S5_GROUP
    noct = S5_GROUPS // gpo
    ost = gpo * S5_STATE
    in_spec = lambda z: pl.BlockSpec((1, gpo) + z.shape[2:], lambda d, o: (d, o, 0, 0, 0, 0))
    out_spec = lambda r, c: pl.BlockSpec((1, 1, r, c), lambda d, o: (d, o, 0, 0))
    out_shape = lambda r, c: jax.ShapeDtypeStruct((2, noct, r, c), BF16)
    return pl.pallas_call(
        _s5_pack_kernel,
        grid=(2, noct),
        in_specs=[in_spec(b4), in_spec(c4), in_spec(k4)],
        out_specs=[out_spec(2 * LANES, 2 * ost), out_spec(2 * ost, 2 * LANES),
                   out_spec(2 * LANES, 2 * LANES)],
        out_shape=[out_shape(2 * LANES, 2 * ost), out_shape(2 * ost, 2 * LANES),
                   out_shape(2 * LANES, 2 * LANES)],
        compiler_params=_cparams(("arbitrary", "arbitrary")),
        name="s5_pack",
    )(b4, c4, k4)


def _s5_direction(x_pair, mod_ref, ng_ref, a2r_ref, a2i_ref, wb_ref, wch_ref, wcu_ref,
                  hr_buf, hi_buf, *, reverse):
    d = D_MODEL
    rows = x_pair[0].shape[0]
    pairs = rows // SUBLANES
    first, second = (1, 0) if reverse else (0, 1)

    mod = mod_ref[...]
    shift, scale = mod[:, 0:d], mod[:, d:2 * d]

    def modulate(x):
        xn = (_rms(x) * ng_ref[...]).reshape(pairs, SUBLANES, d)
        return (xn * (1.0 + scale)[None] + shift[None]).reshape(rows, d)

    u_pair = [modulate(x_pair[0]), modulate(x_pair[1])]
    ub = [u.astype(BF16) for u in u_pair]

    noct = wb_ref.shape[1]
    ost = wb_ref.shape[3] // 2
    if reverse:
        own, prev, carry_src, carry_dst = slice(0, rows), slice(SUBLANES, rows + SUBLANES), \
            slice(0, SUBLANES), slice(rows, rows + SUBLANES)
    else:
        own, prev, carry_src, carry_dst = slice(SUBLANES, rows + SUBLANES), slice(0, rows), \
            slice(rows, rows + SUBLANES), slice(0, SUBLANES)
    hr_buf[carry_dst, :] = hr_buf[carry_src, :]
    hi_buf[carry_dst, :] = hi_buf[carry_src, :]

    def lhs_u(o):
        lanes = slice(o * LANES, (o + 1) * LANES)
        return jnp.concatenate([ub[first][:, lanes], ub[second][:, lanes]], axis=1)

    def project_in(o):
        r = _dot(lhs_u(o), wb_ref[0, o])
        hr_buf[own, o * ost:(o + 1) * ost] = r[:, 0:ost]
        hi_buf[own, o * ost:(o + 1) * ost] = r[:, ost:2 * ost]

    def scan(o):
        for j in range(o * ost, (o + 1) * ost, S5_COLS):
            cols = slice(j, j + S5_COLS)
            ar = a2r_ref[0, :, cols]
            ai = a2i_ref[0, :, cols]
            hr = hr_buf[carry_dst, cols]
            hi = hi_buf[carry_dst, cols]
            for s in range(pairs):
                p = (pairs - 1 - s) if reverse else s
                rws = slice(own.start + p * SUBLANES, own.start + (p + 1) * SUBLANES)
                nr = ar * hr - ai * hi + hr_buf[rws, cols]
                ni = ar * hi + ai * hr + hi_buf[rws, cols]
                hr_buf[rws, cols] = nr
                hi_buf[rws, cols] = ni
                hr, hi = nr, ni

    def project_out(o):
        st = slice(o * ost, (o + 1) * ost)
        return (_dot(hr_buf[prev, st].astype(BF16), wch_ref[0, o, 0:ost, :])
                + _dot(hi_buf[prev, st].astype(BF16), wch_ref[0, o, ost:2 * ost, :])
                + _dot(lhs_u(o), wcu_ref[0, o]))

    outs = [None] * noct
    project_in(0)
    for o in range(noct):
        if o + 1 < noct:
            project_in(o + 1)
        scan(o)
        outs[o] = project_out(o)
    y_pair = [None, None]
    y_pair[first] = jnp.concatenate([r[:, 0:LANES] for r in outs], axis=1)
    y_pair[second] = jnp.concatenate([r[:, LANES:2 * LANES] for r in outs], axis=1)
    return u_pair, y_pair


def _s5_init_state(hr_buf, hi_buf):
    @pl.when(pl.program_id(0) == 0)
    def _():
        hr_buf[...] = jnp.zeros_like(hr_buf)
        hi_buf[...] = jnp.zeros_like(hi_buf)


def _s5_fwd_kernel(x_ref, mod_ref, ng_ref, a2r_ref, a2i_ref, wb_ref, wch_ref, wcu_ref,
                   yf_ref, xtm_ref, hr_buf, hi_buf, xs):
    d = D_MODEL
    pairs = S5_STEPS // 2
    rows = pairs * SUBLANES
    _s5_init_state(hr_buf, hi_buf)
    for b in range(SUBLANES):
        xs[:, b, :] = x_ref[b]
    for sub in range(S5_SUBTILES):
        x4 = xs[sub * S5_STEPS:(sub + 1) * S5_STEPS].reshape(pairs, 2, SUBLANES, d)
        xtm_ref[sub * pairs:(sub + 1) * pairs] = x4
        x_pair = [x4[:, ph].reshape(rows, d) for ph in range(2)]
        _, y_pair = _s5_direction(x_pair, mod_ref, ng_ref, a2r_ref, a2i_ref, wb_ref, wch_ref,
                                  wcu_ref, hr_buf, hi_buf, reverse=False)
        for ph in range(2):
            yf_ref[sub * pairs:(sub + 1) * pairs, ph] = y_pair[ph].reshape(pairs, SUBLANES, d)


def _s5_bwd_kernel(x_ref, mod_ref, ng_ref, a2r_ref, a2i_ref, wb_ref, wch_ref, wcu_ref, yf_ref,
                   dsk_ref, wa_ref, wg_ref, o_ref, hr_buf, hi_buf):
    d = D_MODEL
    pairs = S5_STEPS // 2
    rows = pairs * SUBLANES
    _s5_init_state(hr_buf, hi_buf)
    gate = mod_ref[...][:, 2 * d:3 * d]
    for sub in reversed(range(S5_SUBTILES)):
        prs = slice(sub * pairs, (sub + 1) * pairs)
        x_pair = [x_ref[prs, ph].reshape(rows, d) for ph in range(2)]
        u_pair, yb_pair = _s5_direction(x_pair, mod_ref, ng_ref, a2r_ref, a2i_ref, wb_ref,
                                        wch_ref, wcu_ref, hr_buf, hi_buf, reverse=True)
        zs = []
        for ph in range(2):
            y = yf_ref[prs, ph].reshape(rows, d) + yb_pair[ph] + dsk_ref[...] * u_pair[ph]
            zs.append(jax.nn.gelu(y).astype(BF16))
        zb = jnp.concatenate(zs, axis=0)
        glu = _dot(zb, wa_ref[...]) * jax.nn.sigmoid(_dot(zb, wg_ref[...]))
        for ph in range(2):
            g = glu[ph * rows:(ph + 1) * rows].reshape(pairs, SUBLANES, d)
            o_ref[prs, ph] = x_pair[ph].reshape(pairs, SUBLANES, d) + gate[None] * g


def _s5_layer(x, mod1, ng, a2r, a2i, wb, wch, wcu, d_skip, w_glu_a, w_glu_b):
    bsz, seq, d = x.shape
    sub_rows = S5_STEPS // 2 * SUBLANES
    tile_steps = S5_STEPS * S5_SUBTILES
    nt = seq // tile_steps
    scratch = [
        pltpu.VMEM((sub_rows + SUBLANES, S5_NSTATE), F32),
        pltpu.VMEM((sub_rows + SUBLANES, S5_NSTATE), F32),
    ]

    def dir_spec(w):
        nd = w.ndim - 1
        return lambda direction: pl.BlockSpec((1,) + w.shape[1:],
                                              lambda i: (direction,) + (0,) * nd,
                                              pipeline_mode=pl.Buffered(1))

    def dir_specs(direction, x_spec):
        return [x_spec, _const_spec((SUBLANES, 6 * d)), _const_spec((1, d))] + [
            dir_spec(w)(direction) for w in (a2r, a2i, wb, wch, wcu)]

    bwd = lambda i: nt - 1 - i
    tm_block = (tile_steps // 2, 2, bsz, d)
    tm_shape = jax.ShapeDtypeStruct((seq // 2, 2, bsz, d), F32)
    yf, x_tm = pl.pallas_call(
        _s5_fwd_kernel,
        grid=(nt,),
        in_specs=dir_specs(0, pl.BlockSpec((bsz, tile_steps, d), lambda i: (0, i, 0))),
        out_specs=[pl.BlockSpec(tm_block, lambda i: (i, 0, 0, 0)),
                   pl.BlockSpec(tm_block, lambda i: (i, 0, 0, 0))],
        out_shape=[tm_shape, tm_shape],
        scratch_shapes=scratch + [pltpu.VMEM((tile_steps, SUBLANES, d), F32)],
        compiler_params=_cparams(("arbitrary",)),
        name="s5_fwd",
    )(x, mod1, ng, a2r, a2i, wb, wch, wcu)
    return pl.pallas_call(
        _s5_bwd_kernel,
        grid=(nt,),
        in_specs=dir_specs(1, pl.BlockSpec(tm_block, lambda i: (bwd(i), 0, 0, 0))) + [
            pl.BlockSpec(tm_block, lambda i: (bwd(i), 0, 0, 0)),
            _const_spec((1, d)),
            _const_spec((d, d)),
            _const_spec((d, d)),
        ],
        out_specs=pl.BlockSpec(tm_block, lambda i: (bwd(i), 0, 0, 0)),
        out_shape=tm_shape,
        scratch_shapes=scratch,
        compiler_params=_cparams(("arbitrary",)),
        name="s5_bwd",
    )(x_tm, mod1, ng, a2r, a2i, wb, wch, wcu, yf, d_skip, w_glu_a, w_glu_b)


def kernel(x, c, norm_g, ada_w, ada_b, w_in, conv_w, conv_b, cln_g, cln_b, w_out, s5_lam_re,
           s5_lam_im, s5_log_step, s5_b_re, s5_b_im, s5_c_re, s5_c_im, s5_d, w_glu_a, w_glu_b,
           w_fc1, w_fc2, norm_f):
    bsz, seq, d = x.shape
    assert d == D_MODEL and bsz == SUBLANES
    assert seq % T_RET == 0 and seq % MLP_STEPS == 0
    assert seq % (S5_STEPS * S5_SUBTILES) == 0
    assert norm_g.shape[0] == 2, "one conv/retention layer followed by one S5 layer"

    mod = _modulation(c, ada_w, ada_b)
    mod0 = mod[0].reshape(bsz, 1, 6 * d)
    mod1 = mod[1]
    row = lambda v: v.reshape(1, -1).astype(F32)

    dmat, w_kf, w_kb, w_qf, w_qb, g_tile = _retention_tables(T_RET)
    cosf, sinf = _rotary_tables(seq)
    a, q, k, v, g, kvf, kvb = _inproj(x, mod0, row(norm_g[0, 0]), w_in[0].astype(BF16),
                                      cosf, sinf, w_kf, w_kb)
    states = _retstate(g_tile, kvf, kvb)
    cw = jnp.concatenate([conv_w[0], jnp.zeros((1, MIX_A), conv_w.dtype)], axis=0)
    x = _mixer0(x, mod0, a, q, k, v, g, states, dmat, w_qf, w_qb, cw, row(conv_b[0]),
                row(cln_g[0]), row(cln_b[0]), w_out[0].astype(BF16), row(norm_g[0, 1]),
                w_fc1[0].astype(BF16), w_fc2[0].astype(BF16))

    s5w = _s5_weights(s5_lam_re[0], s5_lam_im[0], s5_log_step[0], s5_b_re[0],
                      s5_b_im[0], s5_c_re[0], s5_c_im[0])
    x = _s5_layer(x, mod1, row(norm_g[1, 0]), *s5w,
                  row(s5_d[0]), w_glu_a[0].astype(BF16), w_glu_b[0].astype(BF16))
    return _mlp1(x.reshape(seq, bsz, d), mod1.reshape(bsz, 1, 6 * d), row(norm_g[1, 1]),
                 w_fc1[1].astype(BF16), w_fc2[1].astype(BF16), row(norm_f))
```

```python
import numpy as np
import jax
import jax.numpy as jnp
from jax import lax
from jax.experimental import pallas as pl
from jax.experimental.pallas import tpu as pltpu

F32 = jnp.float32
BF16 = jnp.bfloat16

D_MODEL = 1024
MIX_A = D_MODEL // 2
RET_HEADS = 4
RET_HEAD_DIM = 128
RET_WIDTH = RET_HEADS * RET_HEAD_DIM
IN_COLS = 2 * MIX_A + 4 * RET_WIDTH
CONV_WIDTH = 31
CONV_HALO = 16
ROPE_BASE = 10000.0
S5_GROUP = 16
S5_GROUPS = D_MODEL // S5_GROUP
S5_STATE = 64
S5_NSTATE = S5_GROUPS * S5_STATE
D_FF = 4 * D_MODEL
EPS = 1e-6

SUBLANES = 8
LANES = 128
T_RET = 512
INPROJ_ROW_GROUPS = 2
MLP_STEPS = 64
S5_STEPS = 64
S5_SUBTILES = 2
S5_COLS = 512
VMEM_LIMIT = 56 * 1024 * 1024


def _cparams(sem):
    return pltpu.CompilerParams(dimension_semantics=sem, vmem_limit_bytes=VMEM_LIMIT)


def _const_spec(shape):
    nd = len(shape)
    return pl.BlockSpec(shape, lambda *_: (0,) * nd, pipeline_mode=pl.Buffered(1))


def _rms(x):
    return x * lax.rsqrt(jnp.mean(x * x, axis=-1, keepdims=True) + EPS)


def _dot(a, b):
    return jnp.dot(a, b, preferred_element_type=F32)


def _ffn(hb, w1_ref, w2_ref):
    d = D_MODEL
    us = []
    for j in range(0, D_FF, d):
        u = jnp.maximum(_dot(hb, w1_ref[:, j:j + d]), 0.0)
        us.append((u * u).astype(BF16))
    return _dot(jnp.concatenate(us, axis=1), w2_ref[...])


def _mod_kernel(c_ref, w_ref, b_ref, o_ref):
    c = c_ref[...]
    ca = c * jax.nn.sigmoid(c)
    w = w_ref[0]
    ch = ca.astype(BF16)
    cl = (ca - ch.astype(F32)).astype(BF16)
    wh = w.astype(BF16)
    wl = (w - wh.astype(F32)).astype(BF16)
    o_ref[0] = _dot(ch, wh) + _dot(cl, wh) + _dot(ch, wl) + b_ref[0]


def _modulation(c, ada_w, ada_b):
    depth, d, d6 = ada_w.shape
    bsz = c.shape[0]
    nblk = d6 // d
    return pl.pallas_call(
        _mod_kernel,
        grid=(depth, nblk),
        in_specs=[
            pl.BlockSpec((bsz, d), lambda l, j: (0, 0)),
            pl.BlockSpec((1, d, d), lambda l, j: (l, 0, j)),
            pl.BlockSpec((1, 1, d), lambda l, j: (l, 0, j)),
        ],
        out_specs=pl.BlockSpec((1, bsz, d), lambda l, j: (l, 0, j)),
        out_shape=jax.ShapeDtypeStruct((depth, bsz, d6), F32),
        compiler_params=_cparams(("arbitrary", "arbitrary")),
        name="adaln_mod",
    )(c, ada_w, ada_b.reshape(depth, 1, d6))


def _retention_tables(tile):
    h = np.arange(RET_HEADS, dtype=np.float64)
    log_g = np.log1p(-np.exp2(-5.0 - h))
    idx = np.arange(tile, dtype=np.float64)
    dist = np.abs(idx[:, None] - idx[None, :])
    dmat = np.exp(log_g[:, None, None] * dist)

    def lanes(w):
        return np.broadcast_to(w[:, :, None], (RET_HEADS, tile, RET_HEAD_DIM))

    w_kf = lanes(np.exp(log_g[:, None] * (tile - 1.0 - idx)[None, :]))
    w_kb = lanes(np.exp(log_g[:, None] * idx[None, :]))
    w_qf = lanes(np.exp(log_g[:, None] * (idx + 1.0)[None, :]))
    w_qb = lanes(np.exp(log_g[:, None] * (tile - idx)[None, :]))
    g_tile = np.exp(log_g * tile)
    f = lambda a: jnp.asarray(np.ascontiguousarray(a), dtype=F32)
    return f(dmat), f(w_kf), f(w_kb), f(w_qf), f(w_qb), f(g_tile)


def _rotary_tables(seq):
    half = RET_HEAD_DIM // 2
    inv = ROPE_BASE ** (-np.arange(0, RET_HEAD_DIM, 2, dtype=np.float64) / RET_HEAD_DIM)
    ang = np.arange(seq, dtype=np.float64)[:, None] * inv[None, :]
    cos, sin = np.cos(ang), np.sin(ang)
    cosf = np.concatenate([cos, cos], axis=1)
    sinf = np.concatenate([-sin, sin], axis=1)
    assert cosf.shape == (seq, 2 * half)
    return jnp.asarray(cosf, dtype=F32), jnp.asarray(sinf, dtype=F32)


def _inproj_kernel(x_ref, mod_ref, ng_ref, win_ref, cos_ref, sin_ref, wkf_ref, wkb_ref,
                   a_ref, q_ref, k_ref, v_ref, g_ref, kvf_ref, kvb_ref):
    d = D_MODEL
    t = x_ref.shape[1]
    mod = mod_ref[0]
    shift, scale = mod[:, 0:d], mod[:, d:2 * d]
    half = RET_HEAD_DIM // 2
    base = 2 * MIX_A
    tn = (((0,), (0,)), ((), ()))
    kvf = [0.0] * RET_HEADS
    kvb = [0.0] * RET_HEADS
    rg = t // INPROJ_ROW_GROUPS
    for r0 in range(0, t, rg):
        rows = slice(r0, r0 + rg)
        h = _rms(x_ref[0, rows, :]) * ng_ref[...] * (1.0 + scale) + shift
        hb = h.astype(BF16)

        a_val = _dot(hb, win_ref[:, 0:MIX_A])
        a_gate = _dot(hb, win_ref[:, MIX_A:2 * MIX_A])
        a_ref[0, rows, :] = a_val * jax.nn.sigmoid(a_gate)

        cosf = cos_ref[rows, :]
        sinf = sin_ref[rows, :]
        g_ref[0, rows, :] = _dot(
            hb, win_ref[:, base + 3 * RET_WIDTH:base + 4 * RET_WIDTH]).astype(BF16)
        v_all = _dot(hb, win_ref[:, base + 2 * RET_WIDTH:base + 3 * RET_WIDTH]).astype(BF16)
        v_ref[0, rows, :] = v_all
        q_all = _dot(hb, win_ref[:, base:base + RET_WIDTH])
        k_all = _dot(hb, win_ref[:, base + RET_WIDTH:base + 2 * RET_WIDTH])
        for hd in range(RET_HEADS):
            lo = hd * RET_HEAD_DIM
            hi = lo + RET_HEAD_DIM
            qh = q_all[:, lo:hi]
            kh = k_all[:, lo:hi]
            qh = (qh * cosf + pltpu.roll(qh, half, 1) * sinf) * (RET_HEAD_DIM ** -0.5)
            kh = kh * cosf + pltpu.roll(kh, half, 1) * sinf
            vb = v_all[:, lo:hi]
            q_ref[0, rows, lo:hi] = qh.astype(BF16)
            k_ref[0, rows, lo:hi] = kh.astype(BF16)
            kvf[hd] = kvf[hd] + lax.dot_general((kh * wkf_ref[hd, rows, :]).astype(BF16), vb, tn,
                                                preferred_element_type=F32)
            kvb[hd] = kvb[hd] + lax.dot_general((kh * wkb_ref[hd, rows, :]).astype(BF16), vb, tn,
                                                preferred_element_type=F32)
    for hd in range(RET_HEADS):
        kvf_ref[0, 0, hd] = kvf[hd]
        kvb_ref[0, 0, hd] = kvb[hd]


def _inproj(x, mod0, ng, w_in, cosf, sinf, w_kf, w_kb):
    bsz, seq, d = x.shape
    t = T_RET
    nt = seq // t
    hd = RET_HEAD_DIM
    row = lambda b, i: (b, i, 0)
    kv_spec = pl.BlockSpec((1, 1, RET_HEADS, hd, hd), lambda b, i: (b, i, 0, 0, 0))
    kv_shape = jax.ShapeDtypeStruct((bsz, nt, RET_HEADS, hd, hd), F32)
    return pl.pallas_call(
        _inproj_kernel,
        grid=(bsz, nt),
        in_specs=[
            pl.BlockSpec((1, t, d), row),
            pl.BlockSpec((1, 1, 6 * d), lambda b, i: (b, 0, 0)),
            _const_spec((1, d)),
            _const_spec((d, IN_COLS)),
            pl.BlockSpec((t, hd), lambda b, i: (i, 0)),
            pl.BlockSpec((t, hd), lambda b, i: (i, 0)),
            _const_spec((RET_HEADS, t, hd)),
            _const_spec((RET_HEADS, t, hd)),
        ],
        out_specs=[
            pl.BlockSpec((1, t, MIX_A), row),
            pl.BlockSpec((1, t, RET_WIDTH), row),
            pl.BlockSpec((1, t, RET_WIDTH), row),
            pl.BlockSpec((1, t, RET_WIDTH), row),
            pl.BlockSpec((1, t, RET_WIDTH), row),
            kv_spec, kv_spec,
        ],
        out_shape=[
            jax.ShapeDtypeStruct((bsz, seq, MIX_A), F32),
            jax.ShapeDtypeStruct((bsz, seq, RET_WIDTH), BF16),
            jax.ShapeDtypeStruct((bsz, seq, RET_WIDTH), BF16),
            jax.ShapeDtypeStruct((bsz, seq, RET_WIDTH), BF16),
            jax.ShapeDtypeStruct((bsz, seq, RET_WIDTH), BF16),
            kv_shape, kv_shape,
        ],
        compiler_params=_cparams(("arbitrary", "arbitrary")),
        name="l0_inproj",
    )(x, mod0, ng, w_in, cosf, sinf, w_kf, w_kb)


def _retstate_kernel(gl_ref, kvf_ref, kvb_ref, s_ref):
    g = gl_ref[pl.program_id(1)]
    nt = kvf_ref.shape[1]
    hd = RET_HEAD_DIM
    s = jnp.zeros((hd, hd), F32)
    for c in range(nt):
        s_ref[0, c, 0, :, 0:hd] = s.astype(BF16)
        s = g * s + kvf_ref[0, c, 0]
    s = jnp.zeros((hd, hd), F32)
    for c in reversed(range(nt)):
        s_ref[0, c, 0, :, hd:2 * hd] = s.astype(BF16)
        s = g * s + kvb_ref[0, c, 0]


def _retstate(g_tile, kvf, kvb):
    bsz, nt, nh, hd, _ = kvf.shape
    kv_spec = pl.BlockSpec((1, nt, 1, hd, hd), lambda b, h: (b, 0, h, 0, 0))
    return pl.pallas_call(
        _retstate_kernel,
        grid=(bsz, nh),
        in_specs=[pl.BlockSpec(memory_space=pltpu.SMEM), kv_spec, kv_spec],
        out_specs=pl.BlockSpec((1, nt, 1, hd, 2 * hd), lambda b, h: (b, 0, h, 0, 0)),
        out_shape=jax.ShapeDtypeStruct((bsz, nt, nh, hd, 2 * hd), BF16),
        compiler_params=_cparams(("arbitrary", "arbitrary")),
        name="l0_retstate",
    )(g_tile, kvf, kvb)


def _mixer0_kernel(x_ref, mod_ref, a_ref, ap_ref, an_ref, q_ref, k_ref, v_ref, g_ref,
                   s_ref, dmat_ref, wqf_ref, wqb_ref, cw_ref, cb_ref, lg_ref, lb_ref, wout_ref,
                   ng2_ref, w1_ref, w2_ref, o_ref, aext, ash, cat):
    d = D_MODEL
    t = a_ref.shape[1]
    i = pl.program_id(1)
    last = pl.num_programs(1) - 1
    halo = CONV_HALO

    aext[0:halo, :] = jnp.where(i > 0, ap_ref[0], 0.0)
    aext[halo:halo + t, :] = a_ref[0]
    aext[halo + t:halo + t + halo, :] = jnp.where(i < last, an_ref[0], 0.0)

    off = halo - CONV_WIDTH // 2
    span = t + SUBLANES * ((off + CONV_WIDTH - 1) // SUBLANES)
    for s in range(1, SUBLANES):
        ash[s - 1, 0:span, :] = aext[s:s + span, :]

    acc = jnp.zeros((t, MIX_A), F32)
    for kk in range(CONV_WIDTH):
        s = (off + kk) % SUBLANES
        base = (off + kk) - s
        tap = aext[base:base + t, :] if s == 0 else ash[s - 1, base:base + t, :]
        acc = acc + tap * cw_ref[kk:kk + 1, :]
    acc = acc + cb_ref[...]
    mu = jnp.mean(acc, axis=-1, keepdims=True)
    xc = acc - mu
    var = jnp.mean(xc * xc, axis=-1, keepdims=True)
    y = xc * lax.rsqrt(var + EPS) * lg_ref[...] + lb_ref[...]
    cat[:, 0:MIX_A] = (y * jax.nn.sigmoid(y)).astype(BF16)

    hd = RET_HEAD_DIM
    nt_dims = (((1,), (1,)), ((), ()))
    for h in range(RET_HEADS):
        lo = h * hd
        qh = q_ref[0, :, lo:lo + hd]
        kh = k_ref[0, :, lo:lo + hd]
        vh = v_ref[0, :, lo:lo + hd]
        sc = lax.dot_general(qh, kh, nt_dims, preferred_element_type=F32)
        p = (sc * dmat_ref[h]).astype(BF16)
        r = _dot(p, vh)
        cross = _dot(qh, s_ref[0, 0, h])
        r = r + wqf_ref[h] * cross[:, 0:hd] + wqb_ref[h] * cross[:, hd:2 * hd]
        r = _rms(r)
        gh = g_ref[0, :, lo:lo + hd].astype(F32)
        cat[:, MIX_A + lo:MIX_A + lo + hd] = (gh * jax.nn.sigmoid(gh) * r).astype(BF16)

    mod = mod_ref[0]
    x1 = x_ref[0] + mod[:, 2 * d:3 * d] * _dot(cat[...], wout_ref[...])

    shift, scale, gate = mod[:, 3 * d:4 * d], mod[:, 4 * d:5 * d], mod[:, 5 * d:6 * d]
    hb = (_rms(x1) * ng2_ref[...] * (1.0 + scale) + shift).astype(BF16)
    o_ref[0] = x1 + gate * _ffn(hb, w1_ref, w2_ref)


def _mixer0(x, mod0, a, q, k, v, g, states, dmat, w_qf, w_qb, conv_w, conv_b, cln_g, cln_b,
            w_out, ng2, w1, w2):
    bsz, seq, d = x.shape
    t = T_RET
    nt = seq // t
    hd = RET_HEAD_DIM
    hb = t // CONV_HALO
    nhalo = seq // CONV_HALO
    row = lambda b, i: (b, i, 0)
    return pl.pallas_call(
        _mixer0_kernel,
        grid=(bsz, nt),
        in_specs=[
            pl.BlockSpec((1, t, d), row),
            pl.BlockSpec((1, 1, 6 * d), lambda b, i: (b, 0, 0)),
            pl.BlockSpec((1, t, MIX_A), row),
            pl.BlockSpec((1, CONV_HALO, MIX_A), lambda b, i: (b, jnp.maximum(i * hb - 1, 0), 0)),
            pl.BlockSpec((1, CONV_HALO, MIX_A),
                         lambda b, i: (b, jnp.minimum((i + 1) * hb, nhalo - 1), 0)),
            pl.BlockSpec((1, t, RET_WIDTH), row),
            pl.BlockSpec((1, t, RET_WIDTH), row),
            pl.BlockSpec((1, t, RET_WIDTH), row),
            pl.BlockSpec((1, t, RET_WIDTH), row),
            pl.BlockSpec((1, 1, RET_HEADS, hd, 2 * hd), lambda b, i: (b, i, 0, 0, 0)),
            _const_spec((RET_HEADS, t, t)),
            _const_spec((RET_HEADS, t, hd)),
            _const_spec((RET_HEADS, t, hd)),
            _const_spec((CONV_WIDTH + 1, MIX_A)),
            _const_spec((1, MIX_A)),
            _const_spec((1, MIX_A)),
            _const_spec((1, MIX_A)),
            _const_spec((d, d)),
            _const_spec((1, d)),
            _const_spec((d, D_FF)),
            _const_spec((D_FF, d)),
        ],
        out_specs=pl.BlockSpec((1, t, d), row),
        out_shape=jax.ShapeDtypeStruct((bsz, seq, d), F32),
        scratch_shapes=[
            pltpu.VMEM((t + 2 * CONV_HALO, MIX_A), F32),
            pltpu.VMEM((SUBLANES - 1, t + 2 * CONV_HALO, MIX_A), F32),
            pltpu.VMEM((t, d), BF16),
        ],
        compiler_params=_cparams(("arbitrary", "arbitrary")),
        name="l0_mixer_mlp",
    )(x, mod0, a, a, a, q, k, v, g, states, dmat, w_qf, w_qb, conv_w, conv_b, cln_g, cln_b,
      w_out, ng2, w1, w2)


def _mlp1_kernel(x_ref, mod_ref, ng_ref, w1_ref, w2_ref, nf_ref, o_ref, xs):
    d = D_MODEL
    steps = MLP_STEPS
    nb = SUBLANES
    for b in range(nb):
        xs[b * steps:(b + 1) * steps, :] = x_ref[:, b, :]
    x = xs[...]
    xn = _rms(x) * ng_ref[...]
    hs = []
    for b in range(nb):
        mod = mod_ref[b]
        shift, scale = mod[:, 3 * d:4 * d], mod[:, 4 * d:5 * d]
        hs.append((xn[b * steps:(b + 1) * steps] * (1.0 + scale) + shift).astype(BF16))
    acc = _ffn(jnp.concatenate(hs, axis=0), w1_ref, w2_ref)
    for b in range(nb):
        gate = mod_ref[b][:, 5 * d:6 * d]
        y = x[b * steps:(b + 1) * steps] + gate * acc[b * steps:(b + 1) * steps]
        o_ref[b] = _rms(y) * nf_ref[...]


def _mlp1(x_tm, mod, ng, w1, w2, nf):
    seq, bsz, d = x_tm.shape
    steps = MLP_STEPS
    return pl.pallas_call(
        _mlp1_kernel,
        grid=(seq // steps,),
        in_specs=[
            pl.BlockSpec((steps, bsz, d), lambda i: (i, 0, 0)),
            _const_spec((bsz, 1, 6 * d)),
            _const_spec((1, d)),
            _const_spec((d, D_FF)),
            _const_spec((D_FF, d)),
            _const_spec((1, d)),
        ],
        out_specs=pl.BlockSpec((bsz, steps, d), lambda i: (0, i, 0)),
        out_shape=jax.ShapeDtypeStruct((bsz, seq, d), F32),
        scratch_shapes=[pltpu.VMEM((steps * bsz, d), F32)],
        compiler_params=_cparams(("arbitrary",)),
        name="l1_mlp_final",
    )(x_tm, mod, ng, w1, w2, nf)


def _s5_weights(lam_re, lam_im, log_step, b_re, b_im, c_re, c_im):
    lr = jnp.minimum(lam_re.astype(F32), -1e-4)
    li = lam_im.astype(F32)
    dt = jnp.exp(log_step.astype(F32))[..., None]
    mag = jnp.exp(lr * dt)
    ab_re, ab_im = mag * jnp.cos(li * dt), mag * jnp.sin(li * dt)
    den = lr * lr + li * li
    nr, ni = ab_re - 1.0, ab_im
    f_re = (nr * lr + ni * li) / den
    f_im = (ni * lr - nr * li) / den
    br_, bi_ = b_re.astype(F32), b_im.astype(F32)
    bb_re = f_re[..., None] * br_ - f_im[..., None] * bi_
    bb_im = f_re[..., None] * bi_ + f_im[..., None] * br_

    cr, ci = c_re.astype(F32), c_im.astype(F32)

    def cmul(xr, xi, yr, yi):
        return xr * yr - xi * yi, xr * yi + xi * yr

    a2_re, a2_im = cmul(ab_re, ab_im, ab_re, ab_im)
    abb_re, abb_im = cmul(ab_re[..., None], ab_im[..., None], bb_re, bb_im)
    ca_re, ca_im = cmul(cr, ci, ab_re[:, :, None, :], ab_im[:, :, None, :])
    ca2_re, ca2_im = cmul(cr, ci, a2_re[:, :, None, :], a2_im[:, :, None, :])

    def re_cb(br, bi):
        hp = lax.Precision.HIGHEST
        return (jnp.einsum('dgcp,dgpk->dgck', cr, br, precision=hp)
                - jnp.einsum('dgcp,dgpk->dgck', ci, bi, precision=hp))

    k0 = re_cb(bb_re, bb_im)
    k1 = re_cb(abb_re, abb_im)

    def blocks(grid):
        t = lambda z: jnp.swapaxes(z, -1, -2)
        return jnp.stack([jnp.stack([t(z) for z in r], axis=2) for r in grid], axis=2)

    b4 = blocks([[abb_re, abb_im], [bb_re, bb_im]])
    c4 = blocks([[ca_re, ca2_re], [-ca_im, -ca2_im]])
    k4 = blocks([[k0, k1], [jnp.zeros_like(k0), k0]])
    wb, wch, wcu = _s5_pack(b4, c4, k4)
    a2r = jnp.broadcast_to(a2_re.reshape(2, 1, S5_NSTATE), (2, SUBLANES, S5_NSTATE))
    a2i = jnp.broadcast_to(a2_im.reshape(2, 1, S5_NSTATE), (2, SUBLANES, S5_NSTATE))
    return a2r, a2i, wb, wch, wcu


def _s5_pack_kernel(b4_ref, c4_ref, k4_ref, wb_ref, wch_ref, wcu_ref):
    gpo = b4_ref.shape[1]
    ost = gpo * S5_STATE
    wb_ref[...] = jnp.zeros_like(wb_ref)
    wch_ref[...] = jnp.zeros_like(wch_ref)
    wcu_ref[...] = jnp.zeros_like(wcu_ref)
    for g in range(gpo):
        ch = slice(g * S5_GROUP, (g + 1) * S5_GROUP)
        st = slice(g * S5_STATE, (g + 1) * S5_STATE)
        for a in range(2):
            for b in range(2):
                off = lambda s, base: slice(base + s.start, base + s.stop)
                wb_ref[0, 0, off(ch, a * LANES), off(st, b * ost)] = b4_ref[0, g, a, b].astype(BF16)
                wch_ref[0, 0, off(st, a * ost), off(ch, b * LANES)] = c4_ref[0, g, a, b].astype(BF16)
                wcu_ref[0, 0, off(ch, a * LANES), off(ch, b * LANES)] = k4_ref[0, g, a, b].astype(BF16)


def _s5_pack(b4, c4, k4):
    gpo = LANES // S5_GROUP
    noct = S5_GROUPS // gpo
    ost = gpo * S5_STATE
    in_spec = lambda z: pl.BlockSpec((1, gpo) + z.shape[2:], lambda d, o: (d, o, 0, 0, 0, 0))
    out_spec = lambda r, c: pl.BlockSpec((1, 1, r, c), lambda d, o: (d, o, 0, 0))
    out_shape = lambda r, c: jax.ShapeDtypeStruct((2, noct, r, c), BF16)
    return pl.pallas_call(
        _s5_pack_kernel,
        grid=(2, noct),
        in_specs=[in_spec(b4), in_spec(c4), in_spec(k4)],
        out_specs=[out_spec(2 * LANES, 2 * ost), out_spec(2 * ost, 2 * LANES),
                   out_spec(2 * LANES, 2 * LANES)],
        out_shape=[out_shape(2 * LANES, 2 * ost), out_shape(2 * ost, 2 * LANES),
                   out_shape(2 * LANES, 2 * LANES)],
        compiler_params=_cparams(("arbitrary", "arbitrary")),
        name="s5_pack",
    )(b4, c4, k4)


def _s5_direction(x_pair, mod_ref, ng_ref, a2r_ref, a2i_ref, wb_ref, wch_ref, wcu_ref,
                  hr_buf, hi_buf, *, reverse):
    d = D_MODEL
    rows = x_pair[0].shape[0]
    pairs = rows // SUBLANES
    first, second = (1, 0) if reverse else (0, 1)

    mod = mod_ref[...]
    shift, scale = mod[:, 0:d], mod[:, d:2 * d]

    def modulate(x):
        xn = (_rms(x) * ng_ref[...]).reshape(pairs, SUBLANES, d)
        return (xn * (1.0 + scale)[None] + shift[None]).reshape(rows, d)

    u_pair = [modulate(x_pair[0]), modulate(x_pair[1])]
    ub = [u.astype(BF16) for u in u_pair]

    noct = wb_ref.shape[1]
    ost = wb_ref.shape[3] // 2
    if reverse:
        own, prev, carry_src, carry_dst = slice(0, rows), slice(SUBLANES, rows + SUBLANES), \
            slice(0, SUBLANES), slice(rows, rows + SUBLANES)
    else:
        own, prev, carry_src, carry_dst = slice(SUBLANES, rows + SUBLANES), slice(0, rows), \
            slice(rows, rows + SUBLANES), slice(0, SUBLANES)
    hr_buf[carry_dst, :] = hr_buf[carry_src, :]
    hi_buf[carry_dst, :] = hi_buf[carry_src, :]

    def lhs_u(o):
        lanes = slice(o * LANES, (o + 1) * LANES)
        return jnp.concatenate([ub[first][:, lanes], ub[second][:, lanes]], axis=1)

    def project_in(o):
        r = _dot(lhs_u(o), wb_ref[0, o])
        hr_buf[own, o * ost:(o + 1) * ost] = r[:, 0:ost]
        hi_buf[own, o * ost:(o + 1) * ost] = r[:, ost:2 * ost]

    def scan(o):
        for j in range(o * ost, (o + 1) * ost, S5_COLS):
            cols = slice(j, j + S5_COLS)
            ar = a2r_ref[0, :, cols]
            ai = a2i_ref[0, :, cols]
            hr = hr_buf[carry_dst, cols]
            hi = hi_buf[carry_dst, cols]
            for s in range(pairs):
                p = (pairs - 1 - s) if reverse else s
                rws = slice(own.start + p * SUBLANES, own.start + (p + 1) * SUBLANES)
                nr = ar * hr - ai * hi + hr_buf[rws, cols]
                ni = ar * hi + ai * hr + hi_buf[rws, cols]
                hr_buf[rws, cols] = nr
                hi_buf[rws, cols] = ni
                hr, hi = nr, ni

    def project_out(o):
        st = slice(o * ost, (o + 1) * ost)
        return (_dot(hr_buf[prev, st].astype(BF16), wch_ref[0, o, 0:ost, :])
                + _dot(hi_buf[prev, st].astype(BF16), wch_ref[0, o, ost:2 * ost, :])
                + _dot(lhs_u(o), wcu_ref[0, o]))

    outs = [None] * noct
    project_in(0)
    for o in range(noct):
        if o + 1 < noct:
            project_in(o + 1)
        scan(o)
        outs[o] = project_out(o)
    y_pair = [None, None]
    y_pair[first] = jnp.concatenate([r[:, 0:LANES] for r in outs], axis=1)
    y_pair[second] = jnp.concatenate([r[:, LANES:2 * LANES] for r in outs], axis=1)
    return u_pair, y_pair


def _s5_init_state(hr_buf, hi_buf):
    @pl.when(pl.program_id(0) == 0)
    def _():
        hr_buf[...] = jnp.zeros_like(hr_buf)
        hi_buf[...] = jnp.zeros_like(hi_buf)


def _s5_fwd_kernel(x_ref, mod_ref, ng_ref, a2r_ref, a2i_ref, wb_ref, wch_ref, wcu_ref,
                   yf_ref, xtm_ref, hr_buf, hi_buf, xs):
    d = D_MODEL
    pairs = S5_STEPS // 2
    rows = pairs * SUBLANES
    _s5_init_state(hr_buf, hi_buf)
    for b in range(SUBLANES):
        xs[:, b, :] = x_ref[b]
    for sub in range(S5_SUBTILES):
        x4 = xs[sub * S5_STEPS:(sub + 1) * S5_STEPS].reshape(pairs, 2, SUBLANES, d)
        xtm_ref[sub * pairs:(sub + 1) * pairs] = x4
        x_pair = [x4[:, ph].reshape(rows, d) for ph in range(2)]
        _, y_pair = _s5_direction(x_pair, mod_ref, ng_ref, a2r_ref, a2i_ref, wb_ref, wch_ref,
                                  wcu_ref, hr_buf, hi_buf, reverse=False)
        for ph in range(2):
            yf_ref[sub * pairs:(sub + 1) * pairs, ph] = y_pair[ph].reshape(pairs, SUBLANES, d)


def _s5_bwd_kernel(x_ref, mod_ref, ng_ref, a2r_ref, a2i_ref, wb_ref, wch_ref, wcu_ref, yf_ref,
                   dsk_ref, wa_ref, wg_ref, o_ref, hr_buf, hi_buf):
    d = D_MODEL
    pairs = S5_STEPS // 2
    rows = pairs * SUBLANES
    _s5_init_state(hr_buf, hi_buf)
    gate = mod_ref[...][:, 2 * d:3 * d]
    for sub in reversed(range(S5_SUBTILES)):
        prs = slice(sub * pairs, (sub + 1) * pairs)
        x_pair = [x_ref[prs, ph].reshape(rows, d) for ph in range(2)]
        u_pair, yb_pair = _s5_direction(x_pair, mod_ref, ng_ref, a2r_ref, a2i_ref, wb_ref,
                                        wch_ref, wcu_ref, hr_buf, hi_buf, reverse=True)
        zs = []
        for ph in range(2):
            y = yf_ref[prs, ph].reshape(rows, d) + yb_pair[ph] + dsk_ref[...] * u_pair[ph]
            zs.append(jax.nn.gelu(y).astype(BF16))
        zb = jnp.concatenate(zs, axis=0)
        glu = _dot(zb, wa_ref[...]) * jax.nn.sigmoid(_dot(zb, wg_ref[...]))
        for ph in range(2):
            g = glu[ph * rows:(ph + 1) * rows].reshape(pairs, SUBLANES, d)
            o_ref[prs, ph] = x_pair[ph].reshape(pairs, SUBLANES, d) + gate[None] * g


def _s5_layer(x, mod1, ng, a2r, a2i, wb, wch, wcu, d_skip, w_glu_a, w_glu_b):
    bsz, seq, d = x.shape
    sub_rows = S5_STEPS // 2 * SUBLANES
    tile_steps = S5_STEPS * S5_SUBTILES
    nt = seq // tile_steps
    scratch = [
        pltpu.VMEM((sub_rows + SUBLANES, S5_NSTATE), F32),
        pltpu.VMEM((sub_rows + SUBLANES, S5_NSTATE), F32),
    ]

    def dir_spec(w):
        nd = w.ndim - 1
        return lambda direction: pl.BlockSpec((1,) + w.shape[1:],
                                              lambda i: (direction,) + (0,) * nd,
                                              pipeline_mode=pl.Buffered(1))

    def dir_specs(direction, x_spec):
        return [x_spec, _const_spec((SUBLANES, 6 * d)), _const_spec((1, d))] + [
            dir_spec(w)(direction) for w in (a2r, a2i, wb, wch, wcu)]

    bwd = lambda i: nt - 1 - i
    tm_block = (tile_steps // 2, 2, bsz, d)
    tm_shape = jax.ShapeDtypeStruct((seq // 2, 2, bsz, d), F32)
    yf, x_tm = pl.pallas_call(
        _s5_fwd_kernel,
        grid=(nt,),
        in_specs=dir_specs(0, pl.BlockSpec((bsz, tile_steps, d), lambda i: (0, i, 0))),
        out_specs=[pl.BlockSpec(tm_block, lambda i: (i, 0, 0, 0)),
                   pl.BlockSpec(tm_block, lambda i: (i, 0, 0, 0))],
        out_shape=[tm_shape, tm_shape],
        scratch_shapes=scratch + [pltpu.VMEM((tile_steps, SUBLANES, d), F32)],
        compiler_params=_cparams(("arbitrary",)),
        name="s5_fwd",
    )(x, mod1, ng, a2r, a2i, wb, wch, wcu)
    return pl.pallas_call(
        _s5_bwd_kernel,
        grid=(nt,),
        in_specs=dir_specs(1, pl.BlockSpec(tm_block, lambda i: (bwd(i), 0, 0, 0))) + [
            pl.BlockSpec(tm_block, lambda i: (bwd(i), 0, 0, 0)),
            _const_spec((1, d)),
            _const_spec((d, d)),
            _const_spec((d, d)),
        ],
        out_specs=pl.BlockSpec(tm_block, lambda i: (bwd(i), 0, 0, 0)),
        out_shape=tm_shape,
        scratch_shapes=scratch,
        compiler_params=_cparams(("arbitrary",)),
        name="s5_bwd",
    )(x_tm, mod1, ng, a2r, a2i, wb, wch, wcu, yf, d_skip, w_glu_a, w_glu_b)


def kernel(x, c, norm_g, ada_w, ada_b, w_in, conv_w, conv_b, cln_g, cln_b, w_out, s5_lam_re,
           s5_lam_im, s5_log_step, s5_b_re, s5_b_im, s5_c_re, s5_c_im, s5_d, w_glu_a, w_glu_b,
           w_fc1, w_fc2, norm_f):
    bsz, seq, d = x.shape
    assert d == D_MODEL and bsz == SUBLANES
    assert seq % T_RET == 0 and seq % MLP_STEPS == 0
    assert seq % (S5_STEPS * S5_SUBTILES) == 0
    assert norm_g.shape[0] == 2, "one conv/retention layer followed by one S5 layer"

    mod = _modulation(c, ada_w, ada_b)
    mod0 = mod[0].reshape(bsz, 1, 6 * d)
    mod1 = mod[1]
    row = lambda v: v.reshape(1, -1).astype(F32)

    dmat, w_kf, w_kb, w_qf, w_qb, g_tile = _retention_tables(T_RET)
    cosf, sinf = _rotary_tables(seq)
    a, q, k, v, g, kvf, kvb = _inproj(x, mod0, row(norm_g[0, 0]), w_in[0].astype(BF16),
                                      cosf, sinf, w_kf, w_kb)
    states = _retstate(g_tile, kvf, kvb)
    cw = jnp.concatenate([conv_w[0], jnp.zeros((1, MIX_A), conv_w.dtype)], axis=0)
    x = _mixer0(x, mod0, a, q, k, v, g, states, dmat, w_qf, w_qb, cw, row(conv_b[0]),
                row(cln_g[0]), row(cln_b[0]), w_out[0].astype(BF16), row(norm_g[0, 1]),
                w_fc1[0].astype(BF16), w_fc2[0].astype(BF16))

    s5w = _s5_weights(s5_lam_re[0], s5_lam_im[0], s5_log_step[0], s5_b_re[0],
                      s5_b_im[0], s5_c_re[0], s5_c_im[0])
    x = _s5_layer(x, mod1, row(norm_g[1, 0]), *s5w,
                  row(s5_d[0]), w_glu_a[0].astype(BF16), w_glu_b[0].astype(BF16))
    return _mlp1(x.reshape(seq, bsz, d), mod1.reshape(bsz, 1, 6 * d), row(norm_g[1, 1]),
                 w_fc1[1].astype(BF16), w_fc2[1].astype(BF16), row(norm_f))
```

```python
import numpy as np
import jax
import jax.numpy as jnp
from jax import lax
from jax.experimental import pallas as pl
from jax.experimental.pallas import tpu as pltpu

F32 = jnp.float32
BF16 = jnp.bfloat16

D_MODEL = 1024
MIX_A = D_MODEL // 2
RET_HEADS = 4
RET_HEAD_DIM = 128
RET_WIDTH = RET_HEADS * RET_HEAD_DIM
IN_COLS = 2 * MIX_A + 4 * RET_WIDTH
CONV_WIDTH = 31
CONV_HALO = 16
ROPE_BASE = 10000.0
S5_GROUP = 16
S5_GROUPS = D_MODEL // S5_GROUP
S5_STATE = 64
S5_NSTATE = S5_GROUPS * S5_STATE
D_FF = 4 * D_MODEL
EPS = 1e-6

SUBLANES = 8
LANES = 128
T_RET = 512
MLP_STEPS = 64
S5_STEPS = 64
S5_SUBTILES = 2
S5_COLS = 512
VMEM_LIMIT = 56 * 1024 * 1024


def _cparams(sem):
    return pltpu.CompilerParams(dimension_semantics=sem, vmem_limit_bytes=VMEM_LIMIT)


def _const_spec(shape):
    nd = len(shape)
    return pl.BlockSpec(shape, lambda *_: (0,) * nd, pipeline_mode=pl.Buffered(1))


def _rms(x):
    return x * lax.rsqrt(jnp.mean(x * x, axis=-1, keepdims=True) + EPS)


def _dot(a, b):
    return jnp.dot(a, b, preferred_element_type=F32)


def _ffn(hb, w1_ref, w2_ref):
    u = jnp.maximum(_dot(hb, w1_ref[...]), 0.0)
    return _dot((u * u).astype(BF16), w2_ref[...])


def _mod_kernel(c_ref, w_ref, b_ref, o_ref):
    c = c_ref[...]
    ca = c * jax.nn.sigmoid(c)
    w = w_ref[0]
    ch = ca.astype(BF16)
    cl = (ca - ch.astype(F32)).astype(BF16)
    wh = w.astype(BF16)
    wl = (w - wh.astype(F32)).astype(BF16)
    o_ref[0] = _dot(ch, wh) + _dot(cl, wh) + _dot(ch, wl) + b_ref[0]


def _modulation(c, ada_w, ada_b):
    depth, d, d6 = ada_w.shape
    bsz = c.shape[0]
    nblk = d6 // d
    return pl.pallas_call(
        _mod_kernel,
        grid=(depth, nblk),
        in_specs=[
            pl.BlockSpec((bsz, d), lambda l, j: (0, 0)),
            pl.BlockSpec((1, d, d), lambda l, j: (l, 0, j)),
            pl.BlockSpec((1, 1, d), lambda l, j: (l, 0, j)),
        ],
        out_specs=pl.BlockSpec((1, bsz, d), lambda l, j: (l, 0, j)),
        out_shape=jax.ShapeDtypeStruct((depth, bsz, d6), F32),
        compiler_params=_cparams(("arbitrary", "arbitrary")),
        name="adaln_mod",
    )(c, ada_w, ada_b.reshape(depth, 1, d6))


def _retention_tables(tile):
    h = np.arange(RET_HEADS, dtype=np.float64)
    log_g = np.log1p(-np.exp2(-5.0 - h))
    idx = np.arange(tile, dtype=np.float64)
    dist = np.abs(idx[:, None] - idx[None, :])
    dmat = np.exp(log_g[:, None, None] * dist)

    def lanes(w):
        return np.broadcast_to(w[:, :, None], (RET_HEADS, tile, RET_HEAD_DIM))

    w_kf = lanes(np.exp(log_g[:, None] * (tile - 1.0 - idx)[None, :]))
    w_kb = lanes(np.exp(log_g[:, None] * idx[None, :]))
    w_qf = lanes(np.exp(log_g[:, None] * (idx + 1.0)[None, :]))
    w_qb = lanes(np.exp(log_g[:, None] * (tile - idx)[None, :]))
    g_tile = np.exp(log_g * tile)
    f = lambda a: jnp.asarray(np.ascontiguousarray(a), dtype=F32)
    return f(dmat), f(w_kf), f(w_kb), f(w_qf), f(w_qb), f(g_tile)


def _rotary_tables(seq):
    half = RET_HEAD_DIM // 2
    inv = ROPE_BASE ** (-np.arange(0, RET_HEAD_DIM, 2, dtype=np.float64) / RET_HEAD_DIM)
    ang = np.arange(seq, dtype=np.float64)[:, None] * inv[None, :]
    cos, sin = np.cos(ang), np.sin(ang)
    cosf = np.concatenate([cos, cos], axis=1)
    sinf = np.concatenate([-sin, sin], axis=1)
    assert cosf.shape == (seq, 2 * half)
    return jnp.asarray(cosf, dtype=F32), jnp.asarray(sinf, dtype=F32)


def _inproj_kernel(x_ref, mod_ref, ng_ref, win_ref, cos_ref, sin_ref, wkf_ref, wkb_ref,
                   a_ref, q_ref, k_ref, v_ref, g_ref, kvf_ref, kvb_ref):
    d = D_MODEL
    mod = mod_ref[0]
    shift, scale = mod[:, 0:d], mod[:, d:2 * d]
    h = _rms(x_ref[0]) * ng_ref[...] * (1.0 + scale) + shift
    proj = _dot(h.astype(BF16), win_ref[...])
    a_ref[0] = proj[:, 0:MIX_A] * jax.nn.sigmoid(proj[:, MIX_A:2 * MIX_A])

    base = 2 * MIX_A
    g_ref[0] = proj[:, base + 3 * RET_WIDTH:base + 4 * RET_WIDTH].astype(BF16)
    v_all = proj[:, base + 2 * RET_WIDTH:base + 3 * RET_WIDTH].astype(BF16)
    v_ref[0] = v_all
    cosf = cos_ref[...]
    sinf = sin_ref[...]
    half = RET_HEAD_DIM // 2
    tn = (((0,), (0,)), ((), ()))
    for hd in range(RET_HEADS):
        lo = hd * RET_HEAD_DIM
        hi = lo + RET_HEAD_DIM
        qh = proj[:, base + lo:base + hi]
        kh = proj[:, base + RET_WIDTH + lo:base + RET_WIDTH + hi]
        qh = (qh * cosf + pltpu.roll(qh, half, 1) * sinf) * (RET_HEAD_DIM ** -0.5)
        kh = kh * cosf + pltpu.roll(kh, half, 1) * sinf
        vb = v_all[:, lo:hi]
        q_ref[0, :, lo:hi] = qh.astype(BF16)
        k_ref[0, :, lo:hi] = kh.astype(BF16)
        kvf_ref[0, 0, hd] = lax.dot_general((kh * wkf_ref[hd]).astype(BF16), vb, tn,
                                            preferred_element_type=F32)
        kvb_ref[0, 0, hd] = lax.dot_general((kh * wkb_ref[hd]).astype(BF16), vb, tn,
                                            preferred_element_type=F32)


def _inproj(x, mod0, ng, w_in, cosf, sinf, w_kf, w_kb):
    bsz, seq, d = x.shape
    t = T_RET
    nt = seq // t
    hd = RET_HEAD_DIM
    row = lambda b, i: (b, i, 0)
    kv_spec = pl.BlockSpec((1, 1, RET_HEADS, hd, hd), lambda b, i: (b, i, 0, 0, 0))
    kv_shape = jax.ShapeDtypeStruct((bsz, nt, RET_HEADS, hd, hd), F32)
    return pl.pallas_call(
        _inproj_kernel,
        grid=(bsz, nt),
        in_specs=[
            pl.BlockSpec((1, t, d), row),
            pl.BlockSpec((1, 1, 6 * d), lambda b, i: (b, 0, 0)),
            _const_spec((1, d)),
            _const_spec((d, IN_COLS)),
            pl.BlockSpec((t, hd), lambda b, i: (i, 0)),
            pl.BlockSpec((t, hd), lambda b, i: (i, 0)),
            _const_spec((RET_HEADS, t, hd)),
            _const_spec((RET_HEADS, t, hd)),
        ],
        out_specs=[
            pl.BlockSpec((1, t, MIX_A), row),
            pl.BlockSpec((1, t, RET_WIDTH), row),
            pl.BlockSpec((1, t, RET_WIDTH), row),
            pl.BlockSpec((1, t, RET_WIDTH), row),
            pl.BlockSpec((1, t, RET_WIDTH), row),
            kv_spec, kv_spec,
        ],
        out_shape=[
            jax.ShapeDtypeStruct((bsz, seq, MIX_A), F32),
            jax.ShapeDtypeStruct((bsz, seq, RET_WIDTH), BF16),
            jax.ShapeDtypeStruct((bsz, seq, RET_WIDTH), BF16),
            jax.ShapeDtypeStruct((bsz, seq, RET_WIDTH), BF16),
            jax.ShapeDtypeStruct((bsz, seq, RET_WIDTH), BF16),
            kv_shape, kv_shape,
        ],
        compiler_params=_cparams(("arbitrary", "arbitrary")),
        name="l0_inproj",
    )(x, mod0, ng, w_in, cosf, sinf, w_kf, w_kb)


def _retstate_kernel(gl_ref, kvf_ref, kvb_ref, s_ref):
    g = gl_ref[pl.program_id(1)]
    nt = kvf_ref.shape[1]
    hd = RET_HEAD_DIM
    s = jnp.zeros((hd, hd), F32)
    for c in range(nt):
        s_ref[0, c, 0, :, 0:hd] = s.astype(BF16)
        s = g * s + kvf_ref[0, c, 0]
    s = jnp.zeros((hd, hd), F32)
    for c in reversed(range(nt)):
        s_ref[0, c, 0, :, hd:2 * hd] = s.astype(BF16)
        s = g * s + kvb_ref[0, c, 0]


def _retstate(g_tile, kvf, kvb):
    bsz, nt, nh, hd, _ = kvf.shape
    kv_spec = pl.BlockSpec((1, nt, 1, hd, hd), lambda b, h: (b, 0, h, 0, 0))
    return pl.pallas_call(
        _retstate_kernel,
        grid=(bsz, nh),
        in_specs=[pl.BlockSpec(memory_space=pltpu.SMEM), kv_spec, kv_spec],
        out_specs=pl.BlockSpec((1, nt, 1, hd, 2 * hd), lambda b, h: (b, 0, h, 0, 0)),
        out_shape=jax.ShapeDtypeStruct((bsz, nt, nh, hd, 2 * hd), BF16),
        compiler_params=_cparams(("arbitrary", "arbitrary")),
        name="l0_retstate",
    )(g_tile, kvf, kvb)


def _mixer0_kernel(x_ref, mod_ref, a_ref, ap_ref, an_ref, q_ref, k_ref, v_ref, g_ref,
                   s_ref, dmat_ref, wqf_ref, wqb_ref, cw_ref, cb_ref, lg_ref, lb_ref, wout_ref,
                   ng2_ref, w1_ref, w2_ref, o_ref, aext, ash, cat):
    d = D_MODEL
    t = a_ref.shape[1]
    i = pl.program_id(1)
    last = pl.num_programs(1) - 1
    halo = CONV_HALO

    aext[0:halo, :] = jnp.where(i > 0, ap_ref[0], 0.0)
    aext[halo:halo + t, :] = a_ref[0]
    aext[halo + t:halo + t + halo, :] = jnp.where(i < last, an_ref[0], 0.0)

    off = halo - CONV_WIDTH // 2
    span = t + SUBLANES * ((off + CONV_WIDTH - 1) // SUBLANES)
    for s in range(1, SUBLANES):
        ash[s - 1, 0:span, :] = aext[s:s + span, :]

    acc = jnp.zeros((t, MIX_A), F32)
    for kk in range(CONV_WIDTH):
        s = (off + kk) % SUBLANES
        base = (off + kk) - s
        tap = aext[base:base + t, :] if s == 0 else ash[s - 1, base:base + t, :]
        acc = acc + tap * cw_ref[kk:kk + 1, :]
    acc = acc + cb_ref[...]
    mu = jnp.mean(acc, axis=-1, keepdims=True)
    xc = acc - mu
    var = jnp.mean(xc * xc, axis=-1, keepdims=True)
    y = xc * lax.rsqrt(var + EPS) * lg_ref[...] + lb_ref[...]
    cat[:, 0:MIX_A] = (y * jax.nn.sigmoid(y)).astype(BF16)

    hd = RET_HEAD_DIM
    nt_dims = (((1,), (1,)), ((), ()))
    for h in range(RET_HEADS):
        lo = h * hd
        qh = q_ref[0, :, lo:lo + hd]
        kh = k_ref[0, :, lo:lo + hd]
        vh = v_ref[0, :, lo:lo + hd]
        sc = lax.dot_general(qh, kh, nt_dims, preferred_element_type=F32)
        p = (sc * dmat_ref[h]).astype(BF16)
        r = _dot(p, vh)
        cross = _dot(qh, s_ref[0, 0, h])
        r = r + wqf_ref[h] * cross[:, 0:hd] + wqb_ref[h] * cross[:, hd:2 * hd]
        r = _rms(r)
        gh = g_ref[0, :, lo:lo + hd].astype(F32)
        cat[:, MIX_A + lo:MIX_A + lo + hd] = (gh * jax.nn.sigmoid(gh) * r).astype(BF16)

    mod = mod_ref[0]
    x1 = x_ref[0] + mod[:, 2 * d:3 * d] * _dot(cat[...], wout_ref[...])

    shift, scale, gate = mod[:, 3 * d:4 * d], mod[:, 4 * d:5 * d], mod[:, 5 * d:6 * d]
    hb = (_rms(x1) * ng2_ref[...] * (1.0 + scale) + shift).astype(BF16)
    o_ref[0] = x1 + gate * _ffn(hb, w1_ref, w2_ref)


def _mixer0(x, mod0, a, q, k, v, g, states, dmat, w_qf, w_qb, conv_w, conv_b, cln_g, cln_b,
            w_out, ng2, w1, w2):
    bsz, seq, d = x.shape
    t = T_RET
    nt = seq // t
    hd = RET_HEAD_DIM
    hb = t // CONV_HALO
    nhalo = seq // CONV_HALO
    row = lambda b, i: (b, i, 0)
    return pl.pallas_call(
        _mixer0_kernel,
        grid=(bsz, nt),
        in_specs=[
            pl.BlockSpec((1, t, d), row),
            pl.BlockSpec((1, 1, 6 * d), lambda b, i: (b, 0, 0)),
            pl.BlockSpec((1, t, MIX_A), row),
            pl.BlockSpec((1, CONV_HALO, MIX_A), lambda b, i: (b, jnp.maximum(i * hb - 1, 0), 0)),
            pl.BlockSpec((1, CONV_HALO, MIX_A),
                         lambda b, i: (b, jnp.minimum((i + 1) * hb, nhalo - 1), 0)),
            pl.BlockSpec((1, t, RET_WIDTH), row),
            pl.BlockSpec((1, t, RET_WIDTH), row),
            pl.BlockSpec((1, t, RET_WIDTH), row),
            pl.BlockSpec((1, t, RET_WIDTH), row),
            pl.BlockSpec((1, 1, RET_HEADS, hd, 2 * hd), lambda b, i: (b, i, 0, 0, 0)),
            _const_spec((RET_HEADS, t, t)),
            _const_spec((RET_HEADS, t, hd)),
            _const_spec((RET_HEADS, t, hd)),
            _const_spec((CONV_WIDTH + 1, MIX_A)),
            _const_spec((1, MIX_A)),
            _const_spec((1, MIX_A)),
            _const_spec((1, MIX_A)),
            _const_spec((d, d)),
            _const_spec((1, d)),
            _const_spec((d, D_FF)),
            _const_spec((D_FF, d)),
        ],
        out_specs=pl.BlockSpec((1, t, d), row),
        out_shape=jax.ShapeDtypeStruct((bsz, seq, d), F32),
        scratch_shapes=[
            pltpu.VMEM((t + 2 * CONV_HALO, MIX_A), F32),
            pltpu.VMEM((SUBLANES - 1, t + 2 * CONV_HALO, MIX_A), F32),
            pltpu.VMEM((t, d), BF16),
        ],
        compiler_params=_cparams(("arbitrary", "arbitrary")),
        name="l0_mixer_mlp",
    )(x, mod0, a, a, a, q, k, v, g, states, dmat, w_qf, w_qb, conv_w, conv_b, cln_g, cln_b,
      w_out, ng2, w1, w2)


def _mlp1_kernel(x_ref, mod_ref, ng_ref, w1_ref, w2_ref, nf_ref, o_ref, xs):
    d = D_MODEL
    steps = MLP_STEPS
    nb = SUBLANES
    for b in range(nb):
        xs[b * steps:(b + 1) * steps, :] = x_ref[:, b, :]
    x = xs[...]
    xn = _rms(x) * ng_ref[...]
    hs = []
    for b in range(nb):
        mod = mod_ref[b]
        shift, scale = mod[:, 3 * d:4 * d], mod[:, 4 * d:5 * d]
        hs.append((xn[b * steps:(b + 1) * steps] * (1.0 + scale) + shift).astype(BF16))
    acc = _ffn(jnp.concatenate(hs, axis=0), w1_ref, w2_ref)
    for b in range(nb):
        gate = mod_ref[b][:, 5 * d:6 * d]
        y = x[b * steps:(b + 1) * steps] + gate * acc[b * steps:(b + 1) * steps]
        o_ref[b] = _rms(y) * nf_ref[...]


def _mlp1(x_tm, mod, ng, w1, w2, nf):
    seq, bsz, d = x_tm.shape
    steps = MLP_STEPS
    return pl.pallas_call(
        _mlp1_kernel,
        grid=(seq // steps,),
        in_specs=[
            pl.BlockSpec((steps, bsz, d), lambda i: (i, 0, 0)),
            _const_spec((bsz, 1, 6 * d)),
            _const_spec((1, d)),
            _const_spec((d, D_FF)),
            _const_spec((D_FF, d)),
            _const_spec((1, d)),
        ],
        out_specs=pl.BlockSpec((bsz, steps, d), lambda i: (0, i, 0)),
        out_shape=jax.ShapeDtypeStruct((bsz, seq, d), F32),
        scratch_shapes=[pltpu.VMEM((steps * bsz, d), F32)],
        compiler_params=_cparams(("arbitrary",)),
        name="l1_mlp_final",
    )(x_tm, mod, ng, w1, w2, nf)


def _s5_weights(lam_re, lam_im, log_step, b_re, b_im, c_re, c_im):
    lr = jnp.minimum(lam_re.astype(F32), -1e-4)
    li = lam_im.astype(F32)
    dt = jnp.exp(log_step.astype(F32))[..., None]
    mag = jnp.exp(lr * dt)
    ab_re, ab_im = mag * jnp.cos(li * dt), mag * jnp.sin(li * dt)
    den = lr * lr + li * li
    nr, ni = ab_re - 1.0, ab_im
    f_re = (nr * lr + ni * li) / den
    f_im = (ni * lr - nr * li) / den
    br_, bi_ = b_re.astype(F32), b_im.astype(F32)
    bb_re = f_re[..., None] * br_ - f_im[..., None] * bi_
    bb_im = f_re[..., None] * bi_ + f_im[..., None] * br_

    cr, ci = c_re.astype(F32), c_im.astype(F32)

    def cmul(xr, xi, yr, yi):
        return xr * yr - xi * yi, xr * yi + xi * yr

    a2_re, a2_im = cmul(ab_re, ab_im, ab_re, ab_im)
    abb_re, abb_im = cmul(ab_re[..., None], ab_im[..., None], bb_re, bb_im)
    ca_re, ca_im = cmul(cr, ci, ab_re[:, :, None, :], ab_im[:, :, None, :])
    ca2_re, ca2_im = cmul(cr, ci, a2_re[:, :, None, :], a2_im[:, :, None, :])

    def re_cb(br, bi):
        hp = lax.Precision.HIGHEST
        return (jnp.einsum('dgcp,dgpk->dgck', cr, br, precision=hp)
                - jnp.einsum('dgcp,dgpk->dgck', ci, bi, precision=hp))

    k0 = re_cb(bb_re, bb_im)
    k1 = re_cb(abb_re, abb_im)

    def blocks(grid):
        t = lambda z: jnp.swapaxes(z, -1, -2)
        return jnp.stack([jnp.stack([t(z) for z in r], axis=2) for r in grid], axis=2)

    b4 = blocks([[abb_re, abb_im], [bb_re, bb_im]])
    c4 = blocks([[ca_re, ca2_re], [-ca_im, -ca2_im]])
    k4 = blocks([[k0, k1], [jnp.zeros_like(k0), k0]])
    wb, wch, wcu = _s5_pack(b4, c4, k4)
    a2r = jnp.broadcast_to(a2_re.reshape(2, 1, S5_NSTATE), (2, SUBLANES, S5_NSTATE))
    a2i = jnp.broadcast_to(a2_im.reshape(2, 1, S5_NSTATE), (2, SUBLANES, S5_NSTATE))
    return a2r, a2i, wb, wch, wcu


def _s5_pack_kernel(b4_ref, c4_ref, k4_ref, wb_ref, wch_ref, wcu_ref):
    gpo = b4_ref.shape[1]
    ost = gpo * S5_STATE
    wb_ref[...] = jnp.zeros_like(wb_ref)
    wch_ref[...] = jnp.zeros_like(wch_ref)
    wcu_ref[...] = jnp.zeros_like(wcu_ref)
    for g in range(gpo):
        ch = slice(g * S5_GROUP, (g + 1) * S5_GROUP)
        st = slice(g * S5_STATE, (g + 1) * S5_STATE)
        for a in range(2):
            for b in range(2):
                off = lambda s, base: slice(base + s.start, base + s.stop)
                wb_ref[0, 0, off(ch, a * LANES), off(st, b * ost)] = b4_ref[0, g, a, b].astype(BF16)
                wch_ref[0, 0, off(st, a * ost), off(ch, b * LANES)] = c4_ref[0, g, a, b].astype(BF16)
                wcu_ref[0, 0, off(ch, a * LANES), off(ch, b * LANES)] = k4_ref[0, g, a, b].astype(BF16)


def _s5_pack(b4, c4, k4):
    gpo = LANES // S5_GROUP
    noct = S5_GROUPS // gpo
    ost = gpo * S5_STATE
    in_spec = lambda z: pl.BlockSpec((1, gpo) + z.shape[2:], lambda d, o: (d, o, 0, 0, 0, 0))
    out_spec = lambda r, c: pl.BlockSpec((1, 1, r, c), lambda d, o: (d, o, 0, 0))
    out_shape = lambda r, c: jax.ShapeDtypeStruct((2, noct, r, c), BF16)
    return pl.pallas_call(
        _s5_pack_kernel,
        grid=(2, noct),
        in_specs=[in_spec(b4), in_spec(c4), in_spec(k4)],
        out_specs=[out_spec(2 * LANES, 2 * ost), out_spec(2 * ost, 2 * LANES),
                   out_spec(2 * LANES, 2 * LANES)],
        out_shape=[out_shape(2 * LANES, 2 * ost), out_shape(2 * ost, 2 * LANES),
                   out_shape(2 * LANES, 2 * LANES)],
        compiler_params=_cparams(("arbitrary", "arbitrary")),
        name="s5_pack",
    )(b4, c4, k4)


def _s5_direction(x_pair, mod_ref, ng_ref, a2r_ref, a2i_ref, wb_ref, wch_ref, wcu_ref,
                  hr_buf, hi_buf, *, reverse):
    d = D_MODEL
    rows = x_pair[0].shape[0]
    pairs = rows // SUBLANES
    first, second = (1, 0) if reverse else (0, 1)

    mod = mod_ref[...]
    shift, scale = mod[:, 0:d], mod[:, d:2 * d]

    def modulate(x):
        xn = (_rms(x) * ng_ref[...]).reshape(pairs, SUBLANES, d)
        return (xn * (1.0 + scale)[None] + shift[None]).reshape(rows, d)

    u_pair = [modulate(x_pair[0]), modulate(x_pair[1])]
    ub = [u.astype(BF16) for u in u_pair]

    noct = wb_ref.shape[1]
    ost = wb_ref.shape[3] // 2
    if reverse:
        own, prev, carry_src, carry_dst = slice(0, rows), slice(SUBLANES, rows + SUBLANES), \
            slice(0, SUBLANES), slice(rows, rows + SUBLANES)
    else:
        own, prev, carry_src, carry_dst = slice(SUBLANES, rows + SUBLANES), slice(0, rows), \
            slice(rows, rows + SUBLANES), slice(0, SUBLANES)
    hr_buf[carry_dst, :] = hr_buf[carry_src, :]
    hi_buf[carry_dst, :] = hi_buf[carry_src, :]

    def lhs_u(o):
        lanes = slice(o * LANES, (o + 1) * LANES)
        return jnp.concatenate([ub[first][:, lanes], ub[second][:, lanes]], axis=1)

    def project_in(o):
        r = _dot(lhs_u(o), wb_ref[0, o])
        hr_buf[own, o * ost:(o + 1) * ost] = r[:, 0:ost]
        hi_buf[own, o * ost:(o + 1) * ost] = r[:, ost:2 * ost]

    def scan(o):
        for j in range(o * ost, (o + 1) * ost, S5_COLS):
            cols = slice(j, j + S5_COLS)
            ar = a2r_ref[0, :, cols]
            ai = a2i_ref[0, :, cols]
            hr = hr_buf[carry_dst, cols]
            hi = hi_buf[carry_dst, cols]
            for s in range(pairs):
                p = (pairs - 1 - s) if reverse else s
                rws = slice(own.start + p * SUBLANES, own.start + (p + 1) * SUBLANES)
                nr = ar * hr - ai * hi + hr_buf[rws, cols]
                ni = ar * hi + ai * hr + hi_buf[rws, cols]
                hr_buf[rws, cols] = nr
                hi_buf[rws, cols] = ni
                hr, hi = nr, ni

    def project_out(o):
        st = slice(o * ost, (o + 1) * ost)
        return (_dot(hr_buf[prev, st].astype(BF16), wch_ref[0, o, 0:ost, :])
                + _dot(hi_buf[prev, st].astype(BF16), wch_ref[0, o, ost:2 * ost, :])
                + _dot(lhs_u(o), wcu_ref[0, o]))

    outs = [None] * noct
    project_in(0)
    for o in range(noct):
        if o + 1 < noct:
            project_in(o + 1)
        scan(o)
        outs[o] = project_out(o)
    y_pair = [None, None]
    y_pair[first] = jnp.concatenate([r[:, 0:LANES] for r in outs], axis=1)
    y_pair[second] = jnp.concatenate([r[:, LANES:2 * LANES] for r in outs], axis=1)
    return u_pair, y_pair


def _s5_init_state(hr_buf, hi_buf):
    @pl.when(pl.program_id(0) == 0)
    def _():
        hr_buf[...] = jnp.zeros_like(hr_buf)
        hi_buf[...] = jnp.zeros_like(hi_buf)


def _s5_fwd_kernel(x_ref, mod_ref, ng_ref, a2r_ref, a2i_ref, wb_ref, wch_ref, wcu_ref,
                   yf_ref, xtm_ref, hr_buf, hi_buf, xs):
    d = D_MODEL
    pairs = S5_STEPS // 2
    rows = pairs * SUBLANES
    _s5_init_state(hr_buf, hi_buf)
    for b in range(SUBLANES):
        xs[:, b, :] = x_ref[b]
    for sub in range(S5_SUBTILES):
        x4 = xs[sub * S5_STEPS:(sub + 1) * S5_STEPS].reshape(pairs, 2, SUBLANES, d)
        xtm_ref[sub * pairs:(sub + 1) * pairs] = x4
        x_pair = [x4[:, ph].reshape(rows, d) for ph in range(2)]
        _, y_pair = _s5_direction(x_pair, mod_ref, ng_ref, a2r_ref, a2i_ref, wb_ref, wch_ref,
                                  wcu_ref, hr_buf, hi_buf, reverse=False)
        for ph in range(2):
            yf_ref[sub * pairs:(sub + 1) * pairs, ph] = y_pair[ph].reshape(pairs, SUBLANES, d)


def _s5_bwd_kernel(x_ref, mod_ref, ng_ref, a2r_ref, a2i_ref, wb_ref, wch_ref, wcu_ref, yf_ref,
                   dsk_ref, wglu_ref, o_ref, hr_buf, hi_buf):
    d = D_MODEL
    pairs = S5_STEPS // 2
    rows = pairs * SUBLANES
    _s5_init_state(hr_buf, hi_buf)
    gate = mod_ref[...][:, 2 * d:3 * d]
    for sub in reversed(range(S5_SUBTILES)):
        prs = slice(sub * pairs, (sub + 1) * pairs)
        x_pair = [x_ref[prs, ph].reshape(rows, d) for ph in range(2)]
        u_pair, yb_pair = _s5_direction(x_pair, mod_ref, ng_ref, a2r_ref, a2i_ref, wb_ref,
                                        wch_ref, wcu_ref, hr_buf, hi_buf, reverse=True)
        zs = []
        for ph in range(2):
            y = yf_ref[prs, ph].reshape(rows, d) + yb_pair[ph] + dsk_ref[...] * u_pair[ph]
            zs.append(jax.nn.gelu(y).astype(BF16))
        zb = jnp.concatenate(zs, axis=0)
        ag = _dot(zb, wglu_ref[...])
        glu = ag[:, 0:d] * jax.nn.sigmoid(ag[:, d:2 * d])
        for ph in range(2):
            g = glu[ph * rows:(ph + 1) * rows].reshape(pairs, SUBLANES, d)
            o_ref[prs, ph] = x_pair[ph].reshape(pairs, SUBLANES, d) + gate[None] * g


def _s5_layer(x, mod1, ng, a2r, a2i, wb, wch, wcu, d_skip, w_glu):
    bsz, seq, d = x.shape
    sub_rows = S5_STEPS // 2 * SUBLANES
    tile_steps = S5_STEPS * S5_SUBTILES
    nt = seq // tile_steps
    scratch = [
        pltpu.VMEM((sub_rows + SUBLANES, S5_NSTATE), F32),
        pltpu.VMEM((sub_rows + SUBLANES, S5_NSTATE), F32),
    ]

    def dir_spec(w):
        nd = w.ndim - 1
        return lambda direction: pl.BlockSpec((1,) + w.shape[1:],
                                              lambda i: (direction,) + (0,) * nd,
                                              pipeline_mode=pl.Buffered(1))

    def dir_specs(direction, x_spec):
        return [x_spec, _const_spec((SUBLANES, 6 * d)), _const_spec((1, d))] + [
            dir_spec(w)(direction) for w in (a2r, a2i, wb, wch, wcu)]

    bwd = lambda i: nt - 1 - i
    tm_block = (tile_steps // 2, 2, bsz, d)
    tm_shape = jax.ShapeDtypeStruct((seq // 2, 2, bsz, d), F32)
    yf, x_tm = pl.pallas_call(
        _s5_fwd_kernel,
        grid=(nt,),
        in_specs=dir_specs(0, pl.BlockSpec((bsz, tile_steps, d), lambda i: (0, i, 0))),
        out_specs=[pl.BlockSpec(tm_block, lambda i: (i, 0, 0, 0)),
                   pl.BlockSpec(tm_block, lambda i: (i, 0, 0, 0))],
        out_shape=[tm_shape, tm_shape],
        scratch_shapes=scratch + [pltpu.VMEM((tile_steps, SUBLANES, d), F32)],
        compiler_params=_cparams(("arbitrary",)),
        name="s5_fwd",
    )(x, mod1, ng, a2r, a2i, wb, wch, wcu)
    return pl.pallas_call(
        _s5_bwd_kernel,
        grid=(nt,),
        in_specs=dir_specs(1, pl.BlockSpec(tm_block, lambda i: (bwd(i), 0, 0, 0))) + [
            pl.BlockSpec(tm_block, lambda i: (bwd(i), 0, 0, 0)),
            _const_spec((1, d)),
            _const_spec((d, 2 * d)),
        ],
        out_specs=pl.BlockSpec(tm_block, lambda i: (bwd(i), 0, 0, 0)),
        out_shape=tm_shape,
        scratch_shapes=scratch,
        compiler_params=_cparams(("arbitrary",)),
        name="s5_bwd",
    )(x_tm, mod1, ng, a2r, a2i, wb, wch, wcu, yf, d_skip, w_glu)


def kernel(x, c, norm_g, ada_w, ada_b, w_in, conv_w, conv_b, cln_g, cln_b, w_out, s5_lam_re,
           s5_lam_im, s5_log_step, s5_b_re, s5_b_im, s5_c_re, s5_c_im, s5_d, w_glu_a, w_glu_b,
           w_fc1, w_fc2, norm_f):
    bsz, seq, d = x.shape
    assert d == D_MODEL and bsz == SUBLANES
    assert seq % T_RET == 0 and seq % MLP_STEPS == 0
    assert seq % (S5_STEPS * S5_SUBTILES) == 0
    assert norm_g.shape[0] == 2, "one conv/retention layer followed by one S5 layer"

    mod = _modulation(c, ada_w, ada_b)
    mod0 = mod[0].reshape(bsz, 1, 6 * d)
    mod1 = mod[1]
    row = lambda v: v.reshape(1, -1).astype(F32)

    dmat, w_kf, w_kb, w_qf, w_qb, g_tile = _retention_tables(T_RET)
    cosf, sinf = _rotary_tables(seq)
    a, q, k, v, g, kvf, kvb = _inproj(x, mod0, row(norm_g[0, 0]), w_in[0].astype(BF16),
                                      cosf, sinf, w_kf, w_kb)
    states = _retstate(g_tile, kvf, kvb)
    cw = jnp.concatenate([conv_w[0], jnp.zeros((1, MIX_A), conv_w.dtype)], axis=0)
    x = _mixer0(x, mod0, a, q, k, v, g, states, dmat, w_qf, w_qb, cw, row(conv_b[0]),
                row(cln_g[0]), row(cln_b[0]), w_out[0].astype(BF16), row(norm_g[0, 1]),
                w_fc1[0].astype(BF16), w_fc2[0].astype(BF16))

    s5w = _s5_weights(s5_lam_re[0], s5_lam_im[0], s5_log_step[0], s5_b_re[0],
                      s5_b_im[0], s5_c_re[0], s5_c_im[0])
    x = _s5_layer(x, mod1, row(norm_g[1, 0]), *s5w,
                  row(s5_d[0]),
                  jnp.concatenate([w_glu_a[0], w_glu_b[0]], axis=1).astype(BF16))
    return _mlp1(x.reshape(seq, bsz, d), mod1.reshape(bsz, 1, 6 * d), row(norm_g[1, 1]),
                 w_fc1[1].astype(BF16), w_fc2[1].astype(BF16), row(norm_f))
```

```python
import numpy as np
import jax
import jax.numpy as jnp
from jax import lax
from jax.experimental import pallas as pl
from jax.experimental.pallas import tpu as pltpu

F32 = jnp.float32
BF16 = jnp.bfloat16

D_MODEL = 1024
MIX_A = D_MODEL // 2
RET_HEADS = 4
RET_HEAD_DIM = 128
RET_WIDTH = RET_HEADS * RET_HEAD_DIM
IN_COLS = 2 * MIX_A + 4 * RET_WIDTH
CONV_WIDTH = 31
CONV_HALO = 16
ROPE_BASE = 10000.0
S5_GROUP = 16
S5_GROUPS = D_MODEL // S5_GROUP
S5_STATE = 64
S5_NSTATE = S5_GROUPS * S5_STATE
D_FF = 4 * D_MODEL
EPS = 1e-6

SUBLANES = 8
LANES = 128
T_RET = 512
MLP_STEPS = 64
S5_STEPS = 64
S5_SUBTILES = 2
S5_COLS = 512
VMEM_LIMIT = 56 * 1024 * 1024


def _cparams(sem):
    return pltpu.CompilerParams(dimension_semantics=sem, vmem_limit_bytes=VMEM_LIMIT)


def _const_spec(shape):
    nd = len(shape)
    return pl.BlockSpec(shape, lambda *_: (0,) * nd, pipeline_mode=pl.Buffered(1))


def _rms(x):
    return x * lax.rsqrt(jnp.mean(x * x, axis=-1, keepdims=True) + EPS)


def _dot(a, b):
    return jnp.dot(a, b, preferred_element_type=F32)


def _ffn(hb, w1_ref, w2_ref):
    u = jnp.maximum(_dot(hb, w1_ref[...]), 0.0)
    return _dot((u * u).astype(BF16), w2_ref[...])


def _mod_kernel(c_ref, w_ref, b_ref, o_ref):
    c = c_ref[...]
    ca = c * jax.nn.sigmoid(c)
    w = w_ref[0]
    ch = ca.astype(BF16)
    cl = (ca - ch.astype(F32)).astype(BF16)
    wh = w.astype(BF16)
    wl = (w - wh.astype(F32)).astype(BF16)
    o_ref[0] = _dot(ch, wh) + _dot(cl, wh) + _dot(ch, wl) + b_ref[0]


def _modulation(c, ada_w, ada_b):
    depth, d, d6 = ada_w.shape
    bsz = c.shape[0]
    nblk = d6 // d
    return pl.pallas_call(
        _mod_kernel,
        grid=(depth, nblk),
        in_specs=[
            pl.BlockSpec((bsz, d), lambda l, j: (0, 0)),
            pl.BlockSpec((1, d, d), lambda l, j: (l, 0, j)),
            pl.BlockSpec((1, 1, d), lambda l, j: (l, 0, j)),
        ],
        out_specs=pl.BlockSpec((1, bsz, d), lambda l, j: (l, 0, j)),
        out_shape=jax.ShapeDtypeStruct((depth, bsz, d6), F32),
        compiler_params=_cparams(("arbitrary", "arbitrary")),
        name="adaln_mod",
    )(c, ada_w, ada_b.reshape(depth, 1, d6))


def _retention_tables(tile):
    h = np.arange(RET_HEADS, dtype=np.float64)
    log_g = np.log1p(-np.exp2(-5.0 - h))
    idx = np.arange(tile, dtype=np.float64)
    dist = np.abs(idx[:, None] - idx[None, :])
    dmat = np.exp(log_g[:, None, None] * dist)

    def lanes(w):
        return np.broadcast_to(w[:, :, None], (RET_HEADS, tile, RET_HEAD_DIM))

    w_kf = lanes(np.exp(log_g[:, None] * (tile - 1.0 - idx)[None, :]))
    w_kb = lanes(np.exp(log_g[:, None] * idx[None, :]))
    w_qf = lanes(np.exp(log_g[:, None] * (idx + 1.0)[None, :]))
    w_qb = lanes(np.exp(log_g[:, None] * (tile - idx)[None, :]))
    g_tile = np.exp(log_g * tile)
    f = lambda a: jnp.asarray(np.ascontiguousarray(a), dtype=F32)
    return f(dmat), f(w_kf), f(w_kb), f(w_qf), f(w_qb), f(g_tile)


def _rotary_tables(seq):
    half = RET_HEAD_DIM // 2
    inv = ROPE_BASE ** (-np.arange(0, RET_HEAD_DIM, 2, dtype=np.float64) / RET_HEAD_DIM)
    ang = np.arange(seq, dtype=np.float64)[:, None] * inv[None, :]
    cos, sin = np.cos(ang), np.sin(ang)
    cosf = np.concatenate([cos, cos], axis=1)
    sinf = np.concatenate([-sin, sin], axis=1)
    assert cosf.shape == (seq, 2 * half)
    return jnp.asarray(cosf, dtype=F32), jnp.asarray(sinf, dtype=F32)


def _inproj_kernel(x_ref, mod_ref, ng_ref, win_ref, cos_ref, sin_ref, wkf_ref, wkb_ref,
                   a_ref, q_ref, k_ref, v_ref, g_ref, kvf_ref, kvb_ref):
    d = D_MODEL
    mod = mod_ref[0]
    shift, scale = mod[:, 0:d], mod[:, d:2 * d]
    h = _rms(x_ref[0]) * ng_ref[...] * (1.0 + scale) + shift
    proj = _dot(h.astype(BF16), win_ref[...])
    a_ref[0] = proj[:, 0:MIX_A] * jax.nn.sigmoid(proj[:, MIX_A:2 * MIX_A])

    base = 2 * MIX_A
    g_ref[0] = proj[:, base + 3 * RET_WIDTH:base + 4 * RET_WIDTH].astype(BF16)
    v_all = proj[:, base + 2 * RET_WIDTH:base + 3 * RET_WIDTH].astype(BF16)
    v_ref[0] = v_all
    cosf = cos_ref[...]
    sinf = sin_ref[...]
    half = RET_HEAD_DIM // 2
    tn = (((0,), (0,)), ((), ()))
    for hd in range(RET_HEADS):
        lo = hd * RET_HEAD_DIM
        hi = lo + RET_HEAD_DIM
        qh = proj[:, base + lo:base + hi]
        kh = proj[:, base + RET_WIDTH + lo:base + RET_WIDTH + hi]
        qh = (qh * cosf + pltpu.roll(qh, half, 1) * sinf) * (RET_HEAD_DIM ** -0.5)
        kh = kh * cosf + pltpu.roll(kh, half, 1) * sinf
        vb = v_all[:, lo:hi]
        q_ref[0, :, lo:hi] = qh.astype(BF16)
        k_ref[0, :, lo:hi] = kh.astype(BF16)
        kvf_ref[0, 0, hd] = lax.dot_general((kh * wkf_ref[hd]).astype(BF16), vb, tn,
                                            preferred_element_type=F32)
        kvb_ref[0, 0, hd] = lax.dot_general((kh * wkb_ref[hd]).astype(BF16), vb, tn,
                                            preferred_element_type=F32)


def _inproj(x, mod0, ng, w_in, cosf, sinf, w_kf, w_kb):
    bsz, seq, d = x.shape
    t = T_RET
    nt = seq // t
    hd = RET_HEAD_DIM
    row = lambda b, i: (b, i, 0)
    kv_spec = pl.BlockSpec((1, 1, RET_HEADS, hd, hd), lambda b, i: (b, i, 0, 0, 0))
    kv_shape = jax.ShapeDtypeStruct((bsz, nt, RET_HEADS, hd, hd), F32)
    return pl.pallas_call(
        _inproj_kernel,
        grid=(bsz, nt),
        in_specs=[
            pl.BlockSpec((1, t, d), row),
            pl.BlockSpec((1, 1, 6 * d), lambda b, i: (b, 0, 0)),
            _const_spec((1, d)),
            _const_spec((d, IN_COLS)),
            pl.BlockSpec((t, hd), lambda b, i: (i, 0)),
            pl.BlockSpec((t, hd), lambda b, i: (i, 0)),
            _const_spec((RET_HEADS, t, hd)),
            _const_spec((RET_HEADS, t, hd)),
        ],
        out_specs=[
            pl.BlockSpec((1, t, MIX_A), row),
            pl.BlockSpec((1, t, RET_WIDTH), row),
            pl.BlockSpec((1, t, RET_WIDTH), row),
            pl.BlockSpec((1, t, RET_WIDTH), row),
            pl.BlockSpec((1, t, RET_WIDTH), row),
            kv_spec, kv_spec,
        ],
        out_shape=[
            jax.ShapeDtypeStruct((bsz, seq, MIX_A), F32),
            jax.ShapeDtypeStruct((bsz, seq, RET_WIDTH), BF16),
            jax.ShapeDtypeStruct((bsz, seq, RET_WIDTH), BF16),
            jax.ShapeDtypeStruct((bsz, seq, RET_WIDTH), BF16),
            jax.ShapeDtypeStruct((bsz, seq, RET_WIDTH), BF16),
            kv_shape, kv_shape,
        ],
        compiler_params=_cparams(("arbitrary", "arbitrary")),
        name="l0_inproj",
    )(x, mod0, ng, w_in, cosf, sinf, w_kf, w_kb)


def _retstate_kernel(gl_ref, kvf_ref, kvb_ref, s_ref):
    g = gl_ref[pl.program_id(1)]
    nt = kvf_ref.shape[1]
    hd = RET_HEAD_DIM
    s = jnp.zeros((hd, hd), F32)
    for c in range(nt):
        s_ref[0, c, 0, :, 0:hd] = s.astype(BF16)
        s = g * s + kvf_ref[0, c, 0]
    s = jnp.zeros((hd, hd), F32)
    for c in reversed(range(nt)):
        s_ref[0, c, 0, :, hd:2 * hd] = s.astype(BF16)
        s = g * s + kvb_ref[0, c, 0]


def _retstate(g_tile, kvf, kvb):
    bsz, nt, nh, hd, _ = kvf.shape
    kv_spec = pl.BlockSpec((1, nt, 1, hd, hd), lambda b, h: (b, 0, h, 0, 0))
    return pl.pallas_call(
        _retstate_kernel,
        grid=(bsz, nh),
        in_specs=[pl.BlockSpec(memory_space=pltpu.SMEM), kv_spec, kv_spec],
        out_specs=pl.BlockSpec((1, nt, 1, hd, 2 * hd), lambda b, h: (b, 0, h, 0, 0)),
        out_shape=jax.ShapeDtypeStruct((bsz, nt, nh, hd, 2 * hd), BF16),
        compiler_params=_cparams(("arbitrary", "arbitrary")),
        name="l0_retstate",
    )(g_tile, kvf, kvb)


def _mixer0_kernel(x_ref, mod_ref, a_ref, ap_ref, an_ref, q_ref, k_ref, v_ref, g_ref,
                   s_ref, dmat_ref, wqf_ref, wqb_ref, cw_ref, cb_ref, lg_ref, lb_ref, wout_ref,
                   ng2_ref, w1_ref, w2_ref, o_ref, aext, ash, cat):
    d = D_MODEL
    t = a_ref.shape[1]
    i = pl.program_id(1)
    last = pl.num_programs(1) - 1
    halo = CONV_HALO

    aext[0:halo, :] = jnp.where(i > 0, ap_ref[0], 0.0)
    aext[halo:halo + t, :] = a_ref[0]
    aext[halo + t:halo + t + halo, :] = jnp.where(i < last, an_ref[0], 0.0)

    off = halo - CONV_WIDTH // 2
    span = t + SUBLANES * ((off + CONV_WIDTH - 1) // SUBLANES)
    for s in range(1, SUBLANES):
        ash[s - 1, 0:span, :] = aext[s:s + span, :]

    acc = jnp.zeros((t, MIX_A), F32)
    for kk in range(CONV_WIDTH):
        s = (off + kk) % SUBLANES
        base = (off + kk) - s
        tap = aext[base:base + t, :] if s == 0 else ash[s - 1, base:base + t, :]
        acc = acc + tap * cw_ref[kk:kk + 1, :]
    acc = acc + cb_ref[...]
    mu = jnp.mean(acc, axis=-1, keepdims=True)
    xc = acc - mu
    var = jnp.mean(xc * xc, axis=-1, keepdims=True)
    y = xc * lax.rsqrt(var + EPS) * lg_ref[...] + lb_ref[...]
    cat[:, 0:MIX_A] = (y * jax.nn.sigmoid(y)).astype(BF16)

    hd = RET_HEAD_DIM
    nt_dims = (((1,), (1,)), ((), ()))
    for h in range(RET_HEADS):
        lo = h * hd
        qh = q_ref[0, :, lo:lo + hd]
        kh = k_ref[0, :, lo:lo + hd]
        vh = v_ref[0, :, lo:lo + hd]
        sc = lax.dot_general(qh, kh, nt_dims, preferred_element_type=F32)
        p = (sc * dmat_ref[h]).astype(BF16)
        r = _dot(p, vh)
        cross = _dot(qh, s_ref[0, 0, h])
        r = r + wqf_ref[h] * cross[:, 0:hd] + wqb_ref[h] * cross[:, hd:2 * hd]
        r = _rms(r)
        gh = g_ref[0, :, lo:lo + hd].astype(F32)
        cat[:, MIX_A + lo:MIX_A + lo + hd] = (gh * jax.nn.sigmoid(gh) * r).astype(BF16)

    mod = mod_ref[0]
    x1 = x_ref[0] + mod[:, 2 * d:3 * d] * _dot(cat[...], wout_ref[...])

    shift, scale, gate = mod[:, 3 * d:4 * d], mod[:, 4 * d:5 * d], mod[:, 5 * d:6 * d]
    hb = (_rms(x1) * ng2_ref[...] * (1.0 + scale) + shift).astype(BF16)
    o_ref[0] = x1
    o_ref[0] = o_ref[0] + gate * _ffn(hb, w1_ref, w2_ref)


def _mixer0(x, mod0, a, q, k, v, g, states, dmat, w_qf, w_qb, conv_w, conv_b, cln_g, cln_b,
            w_out, ng2, w1, w2):
    bsz, seq, d = x.shape
    t = T_RET
    nt = seq // t
    hd = RET_HEAD_DIM
    hb = t // CONV_HALO
    nhalo = seq // CONV_HALO
    row = lambda b, i: (b, i, 0)
    return pl.pallas_call(
        _mixer0_kernel,
        grid=(bsz, nt),
        in_specs=[
            pl.BlockSpec((1, t, d), row),
            pl.BlockSpec((1, 1, 6 * d), lambda b, i: (b, 0, 0)),
            pl.BlockSpec((1, t, MIX_A), row),
            pl.BlockSpec((1, CONV_HALO, MIX_A), lambda b, i: (b, jnp.maximum(i * hb - 1, 0), 0)),
            pl.BlockSpec((1, CONV_HALO, MIX_A),
                         lambda b, i: (b, jnp.minimum((i + 1) * hb, nhalo - 1), 0)),
            pl.BlockSpec((1, t, RET_WIDTH), row),
            pl.BlockSpec((1, t, RET_WIDTH), row),
            pl.BlockSpec((1, t, RET_WIDTH), row),
            pl.BlockSpec((1, t, RET_WIDTH), row),
            pl.BlockSpec((1, 1, RET_HEADS, hd, 2 * hd), lambda b, i: (b, i, 0, 0, 0)),
            _const_spec((RET_HEADS, t, t)),
            _const_spec((RET_HEADS, t, hd)),
            _const_spec((RET_HEADS, t, hd)),
            _const_spec((CONV_WIDTH + 1, MIX_A)),
            _const_spec((1, MIX_A)),
            _const_spec((1, MIX_A)),
            _const_spec((1, MIX_A)),
            _const_spec((d, d)),
            _const_spec((1, d)),
            _const_spec((d, D_FF)),
            _const_spec((D_FF, d)),
        ],
        out_specs=pl.BlockSpec((1, t, d), row),
        out_shape=jax.ShapeDtypeStruct((bsz, seq, d), F32),
        scratch_shapes=[
            pltpu.VMEM((t + 2 * CONV_HALO, MIX_A), F32),
            pltpu.VMEM((SUBLANES - 1, t + 2 * CONV_HALO, MIX_A), F32),
            pltpu.VMEM((t, d), BF16),
        ],
        compiler_params=_cparams(("arbitrary", "arbitrary")),
        name="l0_mixer_mlp",
    )(x, mod0, a, a, a, q, k, v, g, states, dmat, w_qf, w_qb, conv_w, conv_b, cln_g, cln_b,
      w_out, ng2, w1, w2)


def _mlp1_kernel(x_ref, mod_ref, ng_ref, w1_ref, w2_ref, nf_ref, o_ref, xs):
    d = D_MODEL
    steps = MLP_STEPS
    nb = SUBLANES
    for b in range(nb):
        xs[b * steps:(b + 1) * steps, :] = x_ref[:, b, :]
    x = xs[...]
    xn = _rms(x) * ng_ref[...]
    hs = []
    for b in range(nb):
        mod = mod_ref[b]
        shift, scale = mod[:, 3 * d:4 * d], mod[:, 4 * d:5 * d]
        hs.append((xn[b * steps:(b + 1) * steps] * (1.0 + scale) + shift).astype(BF16))
    acc = _ffn(jnp.concatenate(hs, axis=0), w1_ref, w2_ref)
    for b in range(nb):
        gate = mod_ref[b][:, 5 * d:6 * d]
        y = xs[b * steps:(b + 1) * steps, :] + gate * acc[b * steps:(b + 1) * steps]
        o_ref[b] = _rms(y) * nf_ref[...]


def _mlp1(x_tm, mod, ng, w1, w2, nf):
    seq, bsz, d = x_tm.shape
    steps = MLP_STEPS
    return pl.pallas_call(
        _mlp1_kernel,
        grid=(seq // steps,),
        in_specs=[
            pl.BlockSpec((steps, bsz, d), lambda i: (i, 0, 0)),
            _const_spec((bsz, 1, 6 * d)),
            _const_spec((1, d)),
            _const_spec((d, D_FF)),
            _const_spec((D_FF, d)),
            _const_spec((1, d)),
        ],
        out_specs=pl.BlockSpec((bsz, steps, d), lambda i: (0, i, 0)),
        out_shape=jax.ShapeDtypeStruct((bsz, seq, d), F32),
        scratch_shapes=[pltpu.VMEM((steps * bsz, d), F32)],
        compiler_params=_cparams(("arbitrary",)),
        name="l1_mlp_final",
    )(x_tm, mod, ng, w1, w2, nf)


def _s5_weights(lam_re, lam_im, log_step, b_re, b_im, c_re, c_im):
    lr = jnp.minimum(lam_re.astype(F32), -1e-4)
    li = lam_im.astype(F32)
    dt = jnp.exp(log_step.astype(F32))[..., None]
    mag = jnp.exp(lr * dt)
    ab_re, ab_im = mag * jnp.cos(li * dt), mag * jnp.sin(li * dt)
    den = lr * lr + li * li
    nr, ni = ab_re - 1.0, ab_im
    f_re = (nr * lr + ni * li) / den
    f_im = (ni * lr - nr * li) / den
    br_, bi_ = b_re.astype(F32), b_im.astype(F32)
    bb_re = f_re[..., None] * br_ - f_im[..., None] * bi_
    bb_im = f_re[..., None] * bi_ + f_im[..., None] * br_

    cr, ci = c_re.astype(F32), c_im.astype(F32)

    def cmul(xr, xi, yr, yi):
        return xr * yr - xi * yi, xr * yi + xi * yr

    a2_re, a2_im = cmul(ab_re, ab_im, ab_re, ab_im)
    abb_re, abb_im = cmul(ab_re[..., None], ab_im[..., None], bb_re, bb_im)
    ca_re, ca_im = cmul(cr, ci, ab_re[:, :, None, :], ab_im[:, :, None, :])
    ca2_re, ca2_im = cmul(cr, ci, a2_re[:, :, None, :], a2_im[:, :, None, :])

    def re_cb(br, bi):
        hp = lax.Precision.HIGHEST
        return (jnp.einsum('dgcp,dgpk->dgck', cr, br, precision=hp)
                - jnp.einsum('dgcp,dgpk->dgck', ci, bi, precision=hp))

    k0 = re_cb(bb_re, bb_im)
    k1 = re_cb(abb_re, abb_im)

    def blocks(grid):
        t = lambda z: jnp.swapaxes(z, -1, -2)
        return jnp.stack([jnp.stack([t(z) for z in r], axis=2) for r in grid], axis=2)

    b4 = blocks([[abb_re, abb_im], [bb_re, bb_im]])
    c4 = blocks([[ca_re, ca2_re], [-ca_im, -ca2_im]])
    k4 = blocks([[k0, k1], [jnp.zeros_like(k0), k0]])
    wb, wch, wcu = _s5_pack(b4, c4, k4)
    a2r = jnp.broadcast_to(a2_re.reshape(2, 1, S5_NSTATE), (2, SUBLANES, S5_NSTATE))
    a2i = jnp.broadcast_to(a2_im.reshape(2, 1, S5_NSTATE), (2, SUBLANES, S5_NSTATE))
    return a2r, a2i, wb, wch, wcu


def _s5_pack_kernel(b4_ref, c4_ref, k4_ref, wb_ref, wch_ref, wcu_ref):
    gpo = b4_ref.shape[1]
    ost = gpo * S5_STATE
    wb_ref[...] = jnp.zeros_like(wb_ref)
    wch_ref[...] = jnp.zeros_like(wch_ref)
    wcu_ref[...] = jnp.zeros_like(wcu_ref)
    for g in range(gpo):
        ch = slice(g * S5_GROUP, (g + 1) * S5_GROUP)
        st = slice(g * S5_STATE, (g + 1) * S5_STATE)
        for a in range(2):
            for b in range(2):
                off = lambda s, base: slice(base + s.start, base + s.stop)
                wb_ref[0, 0, off(ch, a * LANES), off(st, b * ost)] = b4_ref[0, g, a, b].astype(BF16)
                wch_ref[0, 0, off(st, a * ost), off(ch, b * LANES)] = c4_ref[0, g, a, b].astype(BF16)
                wcu_ref[0, 0, off(ch, a * LANES), off(ch, b * LANES)] = k4_ref[0, g, a, b].astype(BF16)


def _s5_pack(b4, c4, k4):
    gpo = LANES // S5_GROUP
    noct = S5_GROUPS // gpo
    ost = gpo * S5_STATE
    in_spec = lambda z: pl.BlockSpec((1, gpo) + z.shape[2:], lambda d, o: (d, o, 0, 0, 0, 0))
    out_spec = lambda r, c: pl.BlockSpec((1, 1, r, c), lambda d, o: (d, o, 0, 0))
    out_shape = lambda r, c: jax.ShapeDtypeStruct((2, noct, r, c), BF16)
    return pl.pallas_call(
        _s5_pack_kernel,
        grid=(2, noct),
        in_specs=[in_spec(b4), in_spec(c4), in_spec(k4)],
        out_specs=[out_spec(2 * LANES, 2 * ost), out_spec(2 * ost, 2 * LANES),
                   out_spec(2 * LANES, 2 * LANES)],
        out_shape=[out_shape(2 * LANES, 2 * ost), out_shape(2 * ost, 2 * LANES),
                   out_shape(2 * LANES, 2 * LANES)],
        compiler_params=_cparams(("arbitrary", "arbitrary")),
        name="s5_pack",
    )(b4, c4, k4)


def _s5_direction(x_pair, mod_ref, ng_ref, a2r_ref, a2i_ref, wb_ref, wch_ref, wcu_ref,
                  hr_buf, hi_buf, *, reverse):
    d = D_MODEL
    rows = x_pair[0].shape[0]
    pairs = rows // SUBLANES
    first, second = (1, 0) if reverse else (0, 1)

    mod = mod_ref[...]
    shift, scale = mod[:, 0:d], mod[:, d:2 * d]

    def modulate(x):
        xn = (_rms(x) * ng_ref[...]).reshape(pairs, SUBLANES, d)
        return (xn * (1.0 + scale)[None] + shift[None]).reshape(rows, d)

    u_pair = [modulate(x_pair[0]), modulate(x_pair[1])]
    ub = [u.astype(BF16) for u in u_pair]

    noct = wb_ref.shape[1]
    ost = wb_ref.shape[3] // 2
    if reverse:
        own, prev, carry_src, carry_dst = slice(0, rows), slice(SUBLANES, rows + SUBLANES), \
            slice(0, SUBLANES), slice(rows, rows + SUBLANES)
    else:
        own, prev, carry_src, carry_dst = slice(SUBLANES, rows + SUBLANES), slice(0, rows), \
            slice(rows, rows + SUBLANES), slice(0, SUBLANES)
    hr_buf[carry_dst, :] = hr_buf[carry_src, :]
    hi_buf[carry_dst, :] = hi_buf[carry_src, :]

    def lhs_u(o):
        lanes = slice(o * LANES, (o + 1) * LANES)
        return jnp.concatenate([ub[first][:, lanes], ub[second][:, lanes]], axis=1)

    def project_in(o):
        r = _dot(lhs_u(o), wb_ref[0, o])
        hr_buf[own, o * ost:(o + 1) * ost] = r[:, 0:ost]
        hi_buf[own, o * ost:(o + 1) * ost] = r[:, ost:2 * ost]

    def scan(o):
        for j in range(o * ost, (o + 1) * ost, S5_COLS):
            cols = slice(j, j + S5_COLS)
            ar = a2r_ref[0, :, cols]
            ai = a2i_ref[0, :, cols]
            hr = hr_buf[carry_dst, cols]
            hi = hi_buf[carry_dst, cols]
            for s in range(pairs):
                p = (pairs - 1 - s) if reverse else s
                rws = slice(own.start + p * SUBLANES, own.start + (p + 1) * SUBLANES)
                nr = ar * hr - ai * hi + hr_buf[rws, cols]
                ni = ar * hi + ai * hr + hi_buf[rws, cols]
                hr_buf[rws, cols] = nr
                hi_buf[rws, cols] = ni
                hr, hi = nr, ni

    def project_out(o):
        st = slice(o * ost, (o + 1) * ost)
        return (_dot(hr_buf[prev, st].astype(BF16), wch_ref[0, o, 0:ost, :])
                + _dot(hi_buf[prev, st].astype(BF16), wch_ref[0, o, ost:2 * ost, :])
                + _dot(lhs_u(o), wcu_ref[0, o]))

    outs = [None] * noct
    project_in(0)
    for o in range(noct):
        if o + 1 < noct:
            project_in(o + 1)
        scan(o)
        outs[o] = project_out(o)
    y_pair = [None, None]
    y_pair[first] = jnp.concatenate([r[:, 0:LANES] for r in outs], axis=1)
    y_pair[second] = jnp.concatenate([r[:, LANES:2 * LANES] for r in outs], axis=1)
    return u_pair, y_pair


def _s5_init_state(hr_buf, hi_buf):
    @pl.when(pl.program_id(0) == 0)
    def _():
        hr_buf[...] = jnp.zeros_like(hr_buf)
        hi_buf[...] = jnp.zeros_like(hi_buf)


def _s5_fwd_kernel(x_ref, mod_ref, ng_ref, a2r_ref, a2i_ref, wb_ref, wch_ref, wcu_ref,
                   yf_ref, xtm_ref, hr_buf, hi_buf, xs):
    d = D_MODEL
    pairs = S5_STEPS // 2
    rows = pairs * SUBLANES
    _s5_init_state(hr_buf, hi_buf)
    for b in range(SUBLANES):
        xs[:, b, :] = x_ref[b]
    for sub in range(S5_SUBTILES):
        x4 = xs[sub * S5_STEPS:(sub + 1) * S5_STEPS].reshape(pairs, 2, SUBLANES, d)
        xtm_ref[sub * pairs:(sub + 1) * pairs] = x4
        x_pair = [x4[:, ph].reshape(rows, d) for ph in range(2)]
        _, y_pair = _s5_direction(x_pair, mod_ref, ng_ref, a2r_ref, a2i_ref, wb_ref, wch_ref,
                                  wcu_ref, hr_buf, hi_buf, reverse=False)
        for ph in range(2):
            yf_ref[sub * pairs:(sub + 1) * pairs, ph] = y_pair[ph].reshape(pairs, SUBLANES, d)


def _s5_bwd_kernel(x_ref, mod_ref, ng_ref, a2r_ref, a2i_ref, wb_ref, wch_ref, wcu_ref, yf_ref,
                   dsk_ref, wglu_ref, o_ref, hr_buf, hi_buf):
    d = D_MODEL
    pairs = S5_STEPS // 2
    rows = pairs * SUBLANES
    _s5_init_state(hr_buf, hi_buf)
    gate = mod_ref[...][:, 2 * d:3 * d]
    for sub in reversed(range(S5_SUBTILES)):
        prs = slice(sub * pairs, (sub + 1) * pairs)
        x_pair = [x_ref[prs, ph].reshape(rows, d) for ph in range(2)]
        u_pair, yb_pair = _s5_direction(x_pair, mod_ref, ng_ref, a2r_ref, a2i_ref, wb_ref,
                                        wch_ref, wcu_ref, hr_buf, hi_buf, reverse=True)
        zs = []
        for ph in range(2):
            y = yf_ref[prs, ph].reshape(rows, d) + yb_pair[ph] + dsk_ref[...] * u_pair[ph]
            zs.append(jax.nn.gelu(y).astype(BF16))
        zb = jnp.concatenate(zs, axis=0)
        ag = _dot(zb, wglu_ref[...])
        glu = ag[:, 0:d] * jax.nn.sigmoid(ag[:, d:2 * d])
        for ph in range(2):
            g = glu[ph * rows:(ph + 1) * rows].reshape(pairs, SUBLANES, d)
            o_ref[prs, ph] = x_pair[ph].reshape(pairs, SUBLANES, d) + gate[None] * g


def _s5_layer(x, mod1, ng, a2r, a2i, wb, wch, wcu, d_skip, w_glu):
    bsz, seq, d = x.shape
    sub_rows = S5_STEPS // 2 * SUBLANES
    tile_steps = S5_STEPS * S5_SUBTILES
    nt = seq // tile_steps
    scratch = [
        pltpu.VMEM((sub_rows + SUBLANES, S5_NSTATE), F32),
        pltpu.VMEM((sub_rows + SUBLANES, S5_NSTATE), F32),
    ]

    def dir_spec(w):
        nd = w.ndim - 1
        return lambda direction: pl.BlockSpec((1,) + w.shape[1:],
                                              lambda i: (direction,) + (0,) * nd,
                                              pipeline_mode=pl.Buffered(1))

    def dir_specs(direction, x_spec):
        return [x_spec, _const_spec((SUBLANES, 6 * d)), _const_spec((1, d))] + [
            dir_spec(w)(direction) for w in (a2r, a2i, wb, wch, wcu)]

    bwd = lambda i: nt - 1 - i
    tm_block = (tile_steps // 2, 2, bsz, d)
    tm_shape = jax.ShapeDtypeStruct((seq // 2, 2, bsz, d), F32)
    yf, x_tm = pl.pallas_call(
        _s5_fwd_kernel,
        grid=(nt,),
        in_specs=dir_specs(0, pl.BlockSpec((bsz, tile_steps, d), lambda i: (0, i, 0))),
        out_specs=[pl.BlockSpec(tm_block, lambda i: (i, 0, 0, 0)),
                   pl.BlockSpec(tm_block, lambda i: (i, 0, 0, 0))],
        out_shape=[tm_shape, tm_shape],
        scratch_shapes=scratch + [pltpu.VMEM((tile_steps, SUBLANES, d), F32)],
        compiler_params=_cparams(("arbitrary",)),
        name="s5_fwd",
    )(x, mod1, ng, a2r, a2i, wb, wch, wcu)
    return pl.pallas_call(
        _s5_bwd_kernel,
        grid=(nt,),
        in_specs=dir_specs(1, pl.BlockSpec(tm_block, lambda i: (bwd(i), 0, 0, 0))) + [
            pl.BlockSpec(tm_block, lambda i: (bwd(i), 0, 0, 0)),
            _const_spec((1, d)),
            _const_spec((d, 2 * d)),
        ],
        out_specs=pl.BlockSpec(tm_block, lambda i: (bwd(i), 0, 0, 0)),
        out_shape=tm_shape,
        scratch_shapes=scratch,
        compiler_params=_cparams(("arbitrary",)),
        name="s5_bwd",
    )(x_tm, mod1, ng, a2r, a2i, wb, wch, wcu, yf, d_skip, w_glu)


def kernel(x, c, norm_g, ada_w, ada_b, w_in, conv_w, conv_b, cln_g, cln_b, w_out, s5_lam_re,
           s5_lam_im, s5_log_step, s5_b_re, s5_b_im, s5_c_re, s5_c_im, s5_d, w_glu_a, w_glu_b,
           w_fc1, w_fc2, norm_f):
    bsz, seq, d = x.shape
    assert d == D_MODEL and bsz == SUBLANES
    assert seq % T_RET == 0 and seq % MLP_STEPS == 0
    assert seq % (S5_STEPS * S5_SUBTILES) == 0
    assert norm_g.shape[0] == 2, "one conv/retention layer followed by one S5 layer"

    mod = _modulation(c, ada_w, ada_b)
    mod0 = mod[0].reshape(bsz, 1, 6 * d)
    mod1 = mod[1]
    row = lambda v: v.reshape(1, -1).astype(F32)

    dmat, w_kf, w_kb, w_qf, w_qb, g_tile = _retention_tables(T_RET)
    cosf, sinf = _rotary_tables(seq)
    a, q, k, v, g, kvf, kvb = _inproj(x, mod0, row(norm_g[0, 0]), w_in[0].astype(BF16),
                                      cosf, sinf, w_kf, w_kb)
    states = _retstate(g_tile, kvf, kvb)
    cw = jnp.concatenate([conv_w[0], jnp.zeros((1, MIX_A), conv_w.dtype)], axis=0)
    x = _mixer0(x, mod0, a, q, k, v, g, states, dmat, w_qf, w_qb, cw, row(conv_b[0]),
                row(cln_g[0]), row(cln_b[0]), w_out[0].astype(BF16), row(norm_g[0, 1]),
                w_fc1[0].astype(BF16), w_fc2[0].astype(BF16))

    s5w = _s5_weights(s5_lam_re[0], s5_lam_im[0], s5_log_step[0], s5_b_re[0],
                      s5_b_im[0], s5_c_re[0], s5_c_im[0])
    x = _s5_layer(x, mod1, row(norm_g[1, 0]), *s5w,
                  row(s5_d[0]),
                  jnp.concatenate([w_glu_a[0], w_glu_b[0]], axis=1).astype(BF16))
    return _mlp1(x.reshape(seq, bsz, d), mod1.reshape(bsz, 1, 6 * d), row(norm_g[1, 1]),
                 w_fc1[1].astype(BF16), w_fc2[1].astype(BF16), row(norm_f))
```
